```python
import jax, jax.numpy as jnp
from jax import lax
import numpy as np

D_MODEL = 2048
BATCH = 8
SEQ = 4096
DEPTH = 1

CTX_LEN = 256
GRID_W = 64

MLA_HEADS = 8
QK_NOPE_DIM = 128
QK_ROPE_DIM = 64
V_HEAD_DIM = 128
Q_LORA_RANK = 512
KV_LORA_RANK = 256
MLA_WIDTH = MLA_HEADS * V_HEAD_DIM
ATTN_SCALE = (QK_NOPE_DIM + QK_ROPE_DIM) ** -0.5
ROPE_BASE = 10000.0
Q_BLOCK = 128

CONV_WIDTH = D_MODEL - MLA_WIDTH
CONV_K = 3

Q_END = Q_LORA_RANK
KV_END = Q_END + KV_LORA_RANK
KR_END = KV_END + QK_ROPE_DIM
IN_COLS = KR_END + 3 * CONV_WIDTH

N_EXPERTS = 32
N_EXPERT_GROUPS = 4
TOPK_GROUPS = 2
TOP_K = 4
EXPERT_FF = 256
SHARED_FF = 512
ROUTED_SCALE = 2.5

N_MOD = 6
NORM_EPS = 1e-6

kernel_name = "hybrid_mla_shortconv_moe_dit_block"


def rmsnorm(x, g):
    xf = x.astype(jnp.float32)
    y = xf * lax.rsqrt(jnp.mean(xf * xf, axis=-1, keepdims=True) + NORM_EPS)
    return (y * g.astype(jnp.float32)).astype(x.dtype)


def modulated_norm(x, g, shift, scale):
    return rmsnorm(x, g) * (1 + scale) + shift


def axial_rope_tables(rows):
    row = jnp.repeat(jnp.arange(rows, dtype=jnp.float32), GRID_W)
    col = jnp.tile(jnp.arange(GRID_W, dtype=jnp.float32), rows)
    nf = QK_ROPE_DIM // 4
    inv = ROPE_BASE ** (-jnp.arange(nf, dtype=jnp.float32) / nf)
    ang = jnp.concatenate([row[:, None] * inv, col[:, None] * inv], axis=-1)
    return jnp.cos(ang), jnp.sin(ang)


def apply_axial_rope(x, cos, sin):
    bshape = (cos.shape[0],) + (1,) * (x.ndim - 3) + (cos.shape[-1],)
    cos = cos.reshape(bshape).astype(x.dtype)
    sin = sin.reshape(bshape).astype(x.dtype)
    nf = QK_ROPE_DIM // 4
    half = QK_ROPE_DIM // 2

    def rot(v, cs, sn):
        v1, v2 = v[..., :nf], v[..., nf:]
        return jnp.concatenate([v1 * cs - v2 * sn, v2 * cs + v1 * sn], axis=-1)

    return jnp.concatenate([rot(x[..., :half], cos[..., :nf], sin[..., :nf]),
                            rot(x[..., half:], cos[..., nf:], sin[..., nf:])], axis=-1)


def mla_query(q_a, p, rope):
    q = rmsnorm(q_a, p["q_a_norm"]) @ p["w_q_b"]
    q = q.reshape(q.shape[:-1] + (MLA_HEADS, QK_NOPE_DIM + QK_ROPE_DIM))
    qn, qr = q[..., :QK_NOPE_DIM], q[..., QK_NOPE_DIM:]
    if rope is not None:
        qr = apply_axial_rope(qr, *rope)
    return qn, qr


def mla_kv(kv_a, k_rope, p, rope):
    kv = rmsnorm(kv_a, p["kv_a_norm"]) @ p["w_kv_b"]
    kv = kv.reshape(kv.shape[:-1] + (MLA_HEADS, QK_NOPE_DIM + V_HEAD_DIM))
    kn, v = kv[..., :QK_NOPE_DIM], kv[..., QK_NOPE_DIM:]
    kr = apply_axial_rope(k_rope, *rope) if rope is not None else k_rope
    return kn, kr, v


def mla_attention(qn, qr, kn, kr, v):
    b, n = qn.shape[0], qn.shape[1]
    nb = n // Q_BLOCK

    def block(q):
        qn_b, qr_b = q
        s = (jnp.einsum("bqhd,bkhd->bhqk", qn_b, kn)
             + jnp.einsum("bqhr,bkr->bhqk", qr_b, kr)) * ATTN_SCALE
        pr = jax.nn.softmax(s.astype(jnp.float32), axis=-1).astype(v.dtype)
        return jnp.einsum("bhqk,bkhd->bqhd", pr, v)

    qn_blocks = qn.reshape(b, nb, Q_BLOCK, MLA_HEADS, QK_NOPE_DIM).swapaxes(0, 1)
    qr_blocks = qr.reshape(b, nb, Q_BLOCK, MLA_HEADS, QK_ROPE_DIM).swapaxes(0, 1)
    o = lax.map(block, (qn_blocks, qr_blocks))
    return o.swapaxes(0, 1).reshape(b, n, MLA_WIDTH)


def gated_short_conv(g_b, g_c, h, w_conv):
    u = g_c * h
    y = lax.conv_general_dilated(u, w_conv[:, None, :], window_strides=(1,),
                                 padding=((CONV_K // 2, CONV_K // 2),),
                                 dimension_numbers=("NWC", "WIO", "NWC"),
                                 feature_group_count=CONV_WIDTH)
    return g_b * y


def merge_head_groups(o_mla, o_conv, p):
    o = jnp.concatenate([rmsnorm(o_mla, p["o_norm_mla"]), rmsnorm(o_conv, p["o_norm_conv"])], axis=-1)
    return o @ p["w_out"]


def route(xf, w_router, bias):
    t = xf.shape[0]
    s = jax.nn.sigmoid((xf @ w_router).astype(jnp.float32))
    sel = s + bias.astype(jnp.float32)
    grp = sel.reshape(t, N_EXPERT_GROUPS, N_EXPERTS // N_EXPERT_GROUPS)
    gscore = lax.top_k(grp, 2)[0].sum(-1)
    _, gidx = lax.top_k(gscore, TOPK_GROUPS)
    gmask = jax.nn.one_hot(gidx, N_EXPERT_GROUPS, dtype=jnp.float32).sum(-2) > 0
    emask = jnp.repeat(gmask, N_EXPERTS // N_EXPERT_GROUPS, axis=-1)
    _, idx = lax.top_k(jnp.where(emask, sel, -jnp.inf), TOP_K)
    w = jnp.take_along_axis(s, idx, axis=-1)
    w = w / jnp.sum(w, axis=-1, keepdims=True) * ROUTED_SCALE
    return idx, w


def moe_ffn(h, p):
    shp = h.shape
    xf = h.reshape(-1, shp[-1])
    idx, w = route(xf, p["w_router"], p["router_bias"])
    comb = jnp.sum(jax.nn.one_hot(idx, N_EXPERTS, dtype=jnp.float32) * w[..., None], axis=1)
    g = jnp.einsum("td,edf->tef", xf, p["w_exp_gate"])
    u = jnp.einsum("td,edf->tef", xf, p["w_exp_up"])
    a = jax.nn.silu(g) * u * comb[..., None].astype(xf.dtype)
    routed = jnp.einsum("tef,efd->td", a, p["w_exp_down"])
    shared = (jax.nn.silu(xf @ p["w_sh_gate"]) * (xf @ p["w_sh_up"])) @ p["w_sh_down"]
    return (routed + shared).reshape(shp)


def context_update(h_c, nc, kn_c, kr_c, v_c, mods_c, p):
    _, _, ga_c, sf_c, scf_c, gf_c = mods_c
    w_in = p["w_in"]
    qn_c, qr_c = mla_query(nc @ w_in[:, :Q_END], p, None)
    o_mla = mla_attention(qn_c, qr_c, kn_c, kr_c, v_c)
    g_b, g_c, hh = jnp.split(nc @ w_in[:, KR_END:], 3, axis=-1)
    o_conv = gated_short_conv(g_b, g_c, hh, p["conv_w"])
    h_c = h_c + ga_c * merge_head_groups(o_mla, o_conv, p)
    return h_c + gf_c * moe_ffn(modulated_norm(h_c, p["ffn_norm"], sf_c, scf_c), p)


def setup_inputs(seed: int = 0) -> dict:
    key = jax.random.key(seed)
    ks = jax.random.split(key, 26)
    f32 = jnp.float32

    def nrm(k, shape, s):
        return jax.random.normal(k, shape, f32) * s

    def gain(k, shape):
        return 1.0 + 0.02 * jax.random.normal(k, shape, f32)

    d = D_MODEL
    return {
        "x": nrm(ks[0], (BATCH, SEQ, d), 1.0),
        "c": nrm(ks[1], (BATCH, d), 1.0),
        "ctx": nrm(ks[2], (BATCH, CTX_LEN, d), 1.0),
        "c_ctx": nrm(ks[3], (d,), 1.0),
        "w_mod": nrm(ks[4], (DEPTH, d, N_MOD * d), 0.5 * d ** -0.5),
        "b_mod": nrm(ks[5], (DEPTH, N_MOD * d), 0.02),
        "attn_norm": gain(ks[6], (DEPTH, d)),
        "w_in": nrm(ks[7], (DEPTH, d, IN_COLS), d ** -0.5),
        "q_a_norm": gain(ks[8], (DEPTH, Q_LORA_RANK)),
        "w_q_b": nrm(ks[9], (DEPTH, Q_LORA_RANK, MLA_HEADS * (QK_NOPE_DIM + QK_ROPE_DIM)), Q_LORA_RANK ** -0.5),
        "kv_a_norm": gain(ks[10], (DEPTH, KV_LORA_RANK)),
        "w_kv_b": nrm(ks[11], (DEPTH, KV_LORA_RANK, MLA_HEADS * (QK_NOPE_DIM + V_HEAD_DIM)), KV_LORA_RANK ** -0.5),
        "conv_w": nrm(ks[12], (DEPTH, CONV_K, CONV_WIDTH), CONV_K ** -0.5),
        "o_norm_mla": gain(ks[13], (DEPTH, MLA_WIDTH)),
        "o_norm_conv": gain(ks[14], (DEPTH, CONV_WIDTH)),
        "w_out": nrm(ks[15], (DEPTH, d, d), d ** -0.5),
        "ffn_norm": gain(ks[16], (DEPTH, d)),
        "w_router": nrm(ks[17], (DEPTH, d, N_EXPERTS), d ** -0.5),
        "router_bias": nrm(ks[18], (DEPTH, N_EXPERTS), 0.01),
        "w_exp_gate": nrm(ks[19], (DEPTH, N_EXPERTS, d, EXPERT_FF), d ** -0.5),
        "w_exp_up": nrm(ks[20], (DEPTH, N_EXPERTS, d, EXPERT_FF), d ** -0.5),
        "w_exp_down": nrm(ks[21], (DEPTH, N_EXPERTS, EXPERT_FF, d), EXPERT_FF ** -0.5),
        "w_sh_gate": nrm(ks[22], (DEPTH, d, SHARED_FF), d ** -0.5),
        "w_sh_up": nrm(ks[23], (DEPTH, d, SHARED_FF), d ** -0.5),
        "w_sh_down": nrm(ks[24], (DEPTH, SHARED_FF, d), SHARED_FF ** -0.5),
        "final_norm": gain(ks[25], (d,)),
    }


def reference(x, c, ctx, c_ctx, w_mod, b_mod, attn_norm, w_in, q_a_norm, w_q_b, kv_a_norm, w_kv_b,
              conv_w, o_norm_mla, o_norm_conv, w_out, ffn_norm, w_router, router_bias,
              w_exp_gate, w_exp_up, w_exp_down, w_sh_gate, w_sh_up, w_sh_down, final_norm):
    n_lat = x.shape[1]
    ROWS = n_lat // GRID_W
    rope = axial_rope_tables(ROWS)
    silu_c = jax.nn.silu(c)
    silu_cc = jax.nn.silu(c_ctx)[None, :]
    h_x, h_c = x, ctx
    for l in range(DEPTH):
        p = {"w_in": w_in[l], "q_a_norm": q_a_norm[l], "w_q_b": w_q_b[l], "kv_a_norm": kv_a_norm[l],
             "w_kv_b": w_kv_b[l], "conv_w": conv_w[l], "o_norm_mla": o_norm_mla[l],
             "o_norm_conv": o_norm_conv[l], "w_out": w_out[l], "ffn_norm": ffn_norm[l],
             "w_router": w_router[l], "router_bias": router_bias[l], "w_exp_gate": w_exp_gate[l],
             "w_exp_up": w_exp_up[l], "w_exp_down": w_exp_down[l], "w_sh_gate": w_sh_gate[l],
             "w_sh_up": w_sh_up[l], "w_sh_down": w_sh_down[l]}
        mods_x = jnp.split((silu_c @ w_mod[l] + b_mod[l])[:, None, :], N_MOD, axis=-1)
        mods_c = jnp.split((silu_cc @ w_mod[l] + b_mod[l])[:, None, :], N_MOD, axis=-1)
        sa_x, sca_x, ga_x, sf_x, scf_x, gf_x = mods_x

        nc = modulated_norm(h_c, attn_norm[l], mods_c[0], mods_c[1])
        pc = nc @ w_in[l][:, Q_END:KR_END]
        kn_c, kr_c, v_c = mla_kv(pc[..., :KV_LORA_RANK], pc[..., KV_LORA_RANK:], p, None)

        nx = modulated_norm(h_x, attn_norm[l], sa_x, sca_x)
        q_a, kv_a, k_rope, g_b, g_c, hh = jnp.split(
            nx @ w_in[l], [Q_END, KV_END, KR_END, KR_END + CONV_WIDTH, KR_END + 2 * CONV_WIDTH], axis=-1)
        qn, qr = mla_query(q_a, p, rope)
        kn_x, kr_x, v_x = mla_kv(kv_a, k_rope, p, rope)
        o_mla = mla_attention(qn, qr,
                              jnp.concatenate([kn_c, kn_x], axis=1),
                              jnp.concatenate([kr_c, kr_x], axis=1),
                              jnp.concatenate([v_c, v_x], axis=1))
        o_conv = gated_short_conv(g_b, g_c, hh, p["conv_w"])
        h_new = h_x + ga_x * merge_head_groups(o_mla, o_conv, p)
        h_new = h_new + gf_x * moe_ffn(modulated_norm(h_new, p["ffn_norm"], sf_x, scf_x), p)

        if l + 1 < DEPTH:
            h_c = context_update(h_c, nc, kn_c, kr_c, v_c, mods_c, p)
        h_x = h_new
    return rmsnorm(h_x, final_norm)
```

```python
import functools

import jax
import jax.numpy as jnp
from jax import lax
from jax.experimental import pallas as pl
from jax.experimental.pallas import tpu as pltpu

F32 = jnp.float32
BF16 = jnp.bfloat16

N_HEADS = 8
D_NOPE = 128
D_ROPE = 64
D_V = 128
D_QK_PAD = 256
Q_RANK = 512
KV_RANK = 256
MLA_W = N_HEADS * D_V
GRID_W = 64
ROPE_BASE = 10000.0
ATTN_SCALE = (D_NOPE + D_ROPE) ** -0.5
N_EXPERTS = 32
N_GROUPS = 4
GROUP_SIZE = N_EXPERTS // N_GROUPS
TOPK_GROUPS = 2
TOP_K = 4
ROUTED_SCALE = 2.5
N_MOD = 6
EPS = 1e-6

C_Q = 0
C_KV = Q_RANK
C_KR = C_KV + KV_RANK
C_GB = C_KR + 128

LANES = 128
BF16_SUBLANES = 16
VMEM_LIMIT = 56 * 1024 * 1024

TM_IN = 512
TQ = 512
TK = 512
TM_POST = 512
TM_MOE = 512
TN_MODS = 1024


def _rms(x, g):
    return x * lax.rsqrt(jnp.mean(x * x, axis=-1, keepdims=True) + EPS) * g


def _silu(x):
    return x / (1.0 + jnp.exp(-x))


def _split_bf16(x):
    hi = x.astype(BF16)
    lo = (x - hi.astype(F32)).astype(BF16)
    return hi, lo


def _dot(a, b):
    return jnp.dot(a, b, preferred_element_type=F32)


def _mods_kernel(a_ref, w_ref, b_ref, o_ref):
    a = _silu(a_ref[...])
    ah, al = _split_bf16(a)
    wh, wl = _split_bf16(w_ref[...])
    o_ref[...] = _dot(ah, wh) + _dot(al, wh) + _dot(ah, wl) + b_ref[...]


def _mods(cc, w_mod, b_mod):
    rows, d = cc.shape
    n = w_mod.shape[1]
    return pl.pallas_call(
        _mods_kernel,
        grid=(n // TN_MODS,),
        in_specs=[pl.BlockSpec((rows, d), lambda j: (0, 0)),
                  pl.BlockSpec((d, TN_MODS), lambda j: (0, j)),
                  pl.BlockSpec((1, TN_MODS), lambda j: (0, j))],
        out_specs=pl.BlockSpec((rows, TN_MODS), lambda j: (0, j)),
        out_shape=jax.ShapeDtypeStruct((rows, n), F32),
        compiler_params=pltpu.CompilerParams(dimension_semantics=("arbitrary",),
                                             vmem_limit_bytes=VMEM_LIMIT),
        name="mods",
    )(cc, w_mod, b_mod)


def _rope_pair(blk, ca, sa):
    return blk * ca + pltpu.roll(blk, D_ROPE, axis=1) * sa


def _inproj_kernel(x_ref, mods_ref, g_ref, win_ref, wq_ref, wkv_ref, qg_ref, kvg_ref, ca_ref, sa_ref,
                   *out_refs, latent):
    x = x_ref[0]
    shift = mods_ref[0, 0:1, :]
    scale = mods_ref[0, 1:2, :]
    nx = (_rms(x, g_ref[...]) * (1.0 + scale) + shift).astype(BF16)
    ca = ca_ref[...]
    sa = sa_ref[...]

    if latent:
        q_ref, k_ref, v_ref, gb_ref, u_ref = out_refs
        q_a = _dot(nx, win_ref[:, C_Q:C_KV])
        qn = _rms(q_a, qg_ref[...]).astype(BF16)
        q = _dot(qn, wq_ref[...]) * ATTN_SCALE
        for h in range(N_HEADS):
            c0 = h * D_QK_PAD
            q_ref[0, :, c0:c0 + D_NOPE] = q[:, c0:c0 + D_NOPE].astype(BF16)
            q_ref[0, :, c0 + D_NOPE:c0 + D_QK_PAD] = _rope_pair(
                q[:, c0 + D_NOPE:c0 + D_QK_PAD], ca, sa).astype(BF16)
        gb_ref[0] = _dot(nx, win_ref[:, C_GB:C_GB + MLA_W]).astype(BF16)
        g_c = _dot(nx, win_ref[:, C_GB + MLA_W:C_GB + 2 * MLA_W])
        hh = _dot(nx, win_ref[:, C_GB + 2 * MLA_W:C_GB + 3 * MLA_W])
        u_ref[0] = (g_c * hh).astype(BF16)
    else:
        k_ref, v_ref = out_refs

    kv_a = _dot(nx, win_ref[:, C_KV:C_KR])
    kvn = _rms(kv_a, kvg_ref[...]).astype(BF16)
    kv = _dot(kvn, wkv_ref[...])
    kr = _rope_pair(_dot(nx, win_ref[:, C_KR:C_GB]), ca, sa).astype(BF16)
    for h in range(N_HEADS):
        c0 = h * D_QK_PAD
        k_ref[0, :, c0:c0 + D_NOPE] = kv[:, h * D_NOPE:(h + 1) * D_NOPE].astype(BF16)
        k_ref[0, :, c0 + D_NOPE:c0 + D_QK_PAD] = kr
    v_ref[0] = kv[:, MLA_W:].astype(BF16)


def _inproj(x, mods, mod_row, g, win, wq, wkv, qg, kvg, ca, sa, *, latent, tm):
    b, s, d = x.shape
    nt = s // tm
    const = lambda shape: pl.BlockSpec(shape, lambda bi, i: (0,) * len(shape),
                                       pipeline_mode=pl.Buffered(1))
    tok = lambda w: pl.BlockSpec((1, tm, w), lambda bi, i: (bi, i, 0))
    in_specs = [tok(d),
                pl.BlockSpec((1, N_MOD, d), lambda bi, i: (mod_row(bi), 0, 0)),
                const(g.shape), const(win.shape), const(wq.shape), const(wkv.shape),
                const(qg.shape), const(kvg.shape),
                pl.BlockSpec((tm, LANES), lambda bi, i: (i, 0)),
                pl.BlockSpec((tm, LANES), lambda bi, i: (i, 0))]
    kq = N_HEADS * D_QK_PAD
    shapes = [(kq, BF16), (MLA_W, BF16)]
    if latent:
        shapes = [(kq, BF16)] + shapes + [(MLA_W, BF16), (MLA_W, BF16)]
    return pl.pallas_call(
        functools.partial(_inproj_kernel, latent=latent),
        grid=(b, nt),
        in_specs=in_specs,
        out_specs=[tok(w) for w, _ in shapes],
        out_shape=[jax.ShapeDtypeStruct((b, s, w), dt) for w, dt in shapes],
        compiler_params=pltpu.CompilerParams(dimension_semantics=("parallel", "arbitrary"),
                                             vmem_limit_bytes=VMEM_LIMIT),
        name="inproj_latent" if latent else "inproj_ctx",
    )(x, mods, g, win, wq, wkv, qg, kvg, ca, sa)


def _attn_kernel(q_ref, kc_ref, vc_ref, kx_ref, vx_ref, o_ref, *, tk):
    q = q_ref[0]
    tq = q.shape[0]

    def step(k, v, carry):
        m, l, acc = carry
        s = lax.dot_general(q, k, (((1,), (1,)), ((), ())), preferred_element_type=F32)
        m_new = jnp.maximum(m, jnp.max(s, axis=-1, keepdims=True))
        alpha = jnp.exp(m - m_new)
        p = jnp.exp(s - m_new)
        l = alpha * l + jnp.sum(p, axis=-1, keepdims=True)
        acc = alpha * acc + _dot(p.astype(BF16), v)
        return m_new, l, acc

    init = (jnp.full((tq, 1), -jnp.inf, F32), jnp.zeros((tq, 1), F32), jnp.zeros((tq, D_V), F32))
    carry = step(kc_ref[0], vc_ref[0], init)

    def body(j, carry):
        off = pl.multiple_of(j * tk, tk)
        return step(kx_ref[0, pl.ds(off, tk), :], vx_ref[0, pl.ds(off, tk), :], carry)

    _, l, acc = lax.fori_loop(0, kx_ref.shape[1] // tk, body, carry)
    o_ref[0] = (acc / l).astype(o_ref.dtype)


def _attention(q, kc, vc, kx, vx, *, tq, tk):
    b, s, _ = q.shape
    lc = kc.shape[1]
    return pl.pallas_call(
        functools.partial(_attn_kernel, tk=tk),
        grid=(b, N_HEADS, s // tq),
        in_specs=[pl.BlockSpec((1, tq, D_QK_PAD), lambda bi, h, i: (bi, i, h)),
                  pl.BlockSpec((1, lc, D_QK_PAD), lambda bi, h, i: (bi, 0, h)),
                  pl.BlockSpec((1, lc, D_V), lambda bi, h, i: (bi, 0, h)),
                  pl.BlockSpec((1, s, D_QK_PAD), lambda bi, h, i: (bi, 0, h)),
                  pl.BlockSpec((1, s, D_V), lambda bi, h, i: (bi, 0, h))],
        out_specs=pl.BlockSpec((1, tq, D_V), lambda bi, h, i: (bi, i, h)),
        out_shape=jax.ShapeDtypeStruct((b, s, MLA_W), BF16),
        compiler_params=pltpu.CompilerParams(
            dimension_semantics=("parallel", "parallel", "arbitrary"), vmem_limit_bytes=VMEM_LIMIT),
        name="attn",
    )(q, kc, vc, kx, vx)


def _route(logits_t, bias):
    e, tm = logits_t.shape
    neg = -jnp.inf
    s = 1.0 / (1.0 + jnp.exp(-logits_t))
    sel = s + bias
    io_g = lax.broadcasted_iota(jnp.int32, (GROUP_SIZE, tm), 0)
    gs = []
    for g in range(N_GROUPS):
        blk = sel[g * GROUP_SIZE:(g + 1) * GROUP_SIZE]
        m1 = jnp.max(blk, axis=0, keepdims=True)
        i1 = jnp.min(jnp.where(blk == m1, io_g, GROUP_SIZE), axis=0, keepdims=True)
        m2 = jnp.max(jnp.where(io_g == i1, neg, blk), axis=0, keepdims=True)
        gs.append(m1 + m2)
    masks = []
    for g in range(N_GROUPS):
        rank = jnp.zeros((1, tm), jnp.int32)
        for o in range(N_GROUPS):
            if o == g:
                continue
            ahead = (gs[o] > gs[g]) | (gs[o] == gs[g]) if o < g else (gs[o] > gs[g])
            rank = rank + ahead.astype(jnp.int32)
        masks.append(jnp.broadcast_to(rank < TOPK_GROUPS, (GROUP_SIZE, tm)))
    emask = jnp.concatenate(masks, axis=0)
    cur = jnp.where(emask, sel, neg)
    io_e = lax.broadcasted_iota(jnp.int32, (e, tm), 0)
    chosen = jnp.zeros((e, tm), jnp.bool_)
    for _ in range(TOP_K):
        m = jnp.max(cur, axis=0, keepdims=True)
        i = jnp.min(jnp.where(cur == m, io_e, e), axis=0, keepdims=True)
        hit = io_e == i
        chosen = chosen | hit
        cur = jnp.where(hit, neg, cur)
    wsum = jnp.sum(jnp.where(chosen, s, 0.0), axis=0, keepdims=True)
    return jnp.where(chosen, s / wsum * ROUTED_SCALE, 0.0)


def _post_kernel(om_ref, gb_ref, u_ref, up_ref, un_ref, x_ref, mods_ref, cw_ref, g1_ref, g2_ref,
                 wout_ref, gf_ref, wrh_ref, wrl_ref, rb_ref, h_ref, xn_ref, comb_ref):
    i = pl.program_id(1)
    nt = pl.num_programs(1)
    tm = u_ref.shape[1]
    u = u_ref[0].astype(F32)
    prev = jnp.where(i > 0, up_ref[0].astype(F32)[BF16_SUBLANES - 1:BF16_SUBLANES], 0.0)
    nxt = jnp.where(i < nt - 1, un_ref[0].astype(F32)[0:1], 0.0)
    row = lax.broadcasted_iota(jnp.int32, u.shape, 0)
    u_m1 = jnp.where(row == 0, prev, pltpu.roll(u, 1, axis=0))
    u_p1 = jnp.where(row == tm - 1, nxt, pltpu.roll(u, tm - 1, axis=0))
    cw = cw_ref[...]
    y = gb_ref[0].astype(F32) * (cw[0:1] * u_m1 + cw[1:2] * u + cw[2:3] * u_p1)
    o = jnp.concatenate([_rms(om_ref[0].astype(F32), g1_ref[...]).astype(BF16),
                         _rms(y, g2_ref[...]).astype(BF16)], axis=-1)
    ga = mods_ref[0, 2:3, :]
    sf = mods_ref[0, 3:4, :]
    scf = mods_ref[0, 4:5, :]
    h = x_ref[0] + ga * _dot(o, wout_ref[...])
    h_ref[0] = h
    xn = _rms(h, gf_ref[...]) * (1.0 + scf) + sf
    xh, xl = _split_bf16(xn)
    xn_ref[0] = xh
    wrh = wrh_ref[...]
    logits = _dot(xh, wrh) + _dot(xl, wrh) + _dot(xh, wrl_ref[...])
    comb_t = _route(logits.T[:N_EXPERTS], rb_ref[...])
    comb_ref[...] = jnp.concatenate(
        [comb_t, jnp.zeros((LANES - N_EXPERTS, tm), F32)], axis=0).T


def _post(o_mla, gb, u, x, mods, conv_w, g1, g2, wout, gf, wrh, wrl, rb, *, tm):
    b, s, d = x.shape
    nt = s // tm
    hb = tm // BF16_SUBLANES
    last = s // BF16_SUBLANES - 1
    const = lambda a: pl.BlockSpec(a.shape, lambda bi, i: (0,) * a.ndim, pipeline_mode=pl.Buffered(1))
    tok = lambda w: pl.BlockSpec((1, tm, w), lambda bi, i: (bi, i, 0))
    in_specs = [tok(MLA_W), tok(MLA_W), tok(MLA_W),
                pl.BlockSpec((1, BF16_SUBLANES, MLA_W), lambda bi, i: (bi, jnp.maximum(i * hb - 1, 0), 0)),
                pl.BlockSpec((1, BF16_SUBLANES, MLA_W), lambda bi, i: (bi, jnp.minimum((i + 1) * hb, last), 0)),
                tok(d),
                pl.BlockSpec((1, N_MOD, d), lambda bi, i: (bi, 0, 0)),
                const(conv_w), const(g1), const(g2), const(wout), const(gf), const(wrh), const(wrl),
                const(rb)]
    return pl.pallas_call(
        _post_kernel,
        grid=(b, nt),
        in_specs=in_specs,
        out_specs=[tok(d), tok(d), pl.BlockSpec((tm, LANES), lambda bi, i: (bi * nt + i, 0))],
        out_shape=[jax.ShapeDtypeStruct((b, s, d), F32), jax.ShapeDtypeStruct((b, s, d), BF16),
                   jax.ShapeDtypeStruct((b * s, LANES), F32)],
        compiler_params=pltpu.CompilerParams(dimension_semantics=("parallel", "arbitrary"),
                                             vmem_limit_bytes=VMEM_LIMIT),
        name="post",
    )(o_mla, gb, u, u, u, x, mods, conv_w, g1, g2, wout, gf, wrh, wrl, rb)


def _moe_kernel(xn_ref, h_ref, comb_ref, mods_ref, wgu_ref, wd_ref, wsgu_ref, wsd_ref, gfin_ref,
                o_ref, acc_ref):
    e = pl.program_id(2)
    xn = xn_ref[0]
    ff = wd_ref.shape[1]

    @pl.when(e == 0)
    def _():
        sgu = _dot(xn, wsgu_ref[...])
        sf = sgu.shape[1] // 2
        acc_ref[...] = _dot((_silu(sgu[:, :sf]) * sgu[:, sf:]).astype(BF16), wsd_ref[...])

    gu = _dot(xn, wgu_ref[0])
    lane = lax.broadcasted_iota(jnp.int32, comb_ref.shape, 1)
    col = jnp.sum(jnp.where(lane == e, comb_ref[...], 0.0), axis=-1, keepdims=True)
    a = _silu(gu[:, :ff]) * gu[:, ff:] * col
    acc_ref[...] += _dot(a.astype(BF16), wd_ref[0])

    @pl.when(e == pl.num_programs(2) - 1)
    def _():
        gate = mods_ref[0, 5:6, :]
        o_ref[0] = _rms(h_ref[0] + gate * acc_ref[...], gfin_ref[...])


def _moe(xn, h, comb, mods, wgu, wd, wsgu, wsd, gfin, *, tm):
    b, s, d = h.shape
    nt = s // tm
    ne = wgu.shape[0]
    const = lambda a: pl.BlockSpec(a.shape, lambda bi, i, e: (0,) * a.ndim, pipeline_mode=pl.Buffered(1))
    tok = lambda w: pl.BlockSpec((1, tm, w), lambda bi, i, e: (bi, i, 0))
    return pl.pallas_call(
        _moe_kernel,
        grid=(b, nt, ne),
        in_specs=[tok(d), tok(d),
                  pl.BlockSpec((tm, LANES), lambda bi, i, e: (bi * nt + i, 0)),
                  pl.BlockSpec((1, N_MOD, d), lambda bi, i, e: (bi, 0, 0)),
                  pl.BlockSpec((1,) + wgu.shape[1:], lambda bi, i, e: (e, 0, 0)),
                  pl.BlockSpec((1,) + wd.shape[1:], lambda bi, i, e: (e, 0, 0)),
                  const(wsgu), const(wsd), const(gfin)],
        out_specs=tok(d),
        out_shape=jax.ShapeDtypeStruct((b, s, d), F32),
        scratch_shapes=[pltpu.VMEM((tm, d), F32)],
        compiler_params=pltpu.CompilerParams(
            dimension_semantics=("parallel", "arbitrary", "arbitrary"), vmem_limit_bytes=VMEM_LIMIT),
        name="moe",
    )(xn, h, comb, mods, wgu, wd, wsgu, wsd, gfin)


def _rope_tables(s):
    nf = D_ROPE // 4
    pos = jnp.arange(s, dtype=jnp.int32)
    row = (pos // GRID_W).astype(F32)
    col = (pos % GRID_W).astype(F32)
    inv = ROPE_BASE ** (-jnp.arange(nf, dtype=F32) / nf)
    ar, ac = row[:, None] * inv, col[:, None] * inv
    z = jnp.zeros((s, LANES - D_ROPE), F32)
    ca = jnp.concatenate([jnp.cos(ar), jnp.cos(ar), jnp.cos(ac), jnp.cos(ac), z], axis=-1)
    sa = jnp.concatenate([-jnp.sin(ar), jnp.sin(ar), -jnp.sin(ac), jnp.sin(ac), z], axis=-1)
    return ca, sa


def _swap_pairs(w):
    nf = D_ROPE // 4
    return jnp.concatenate([w[..., nf:2 * nf], w[..., :nf], w[..., 3 * nf:], w[..., 2 * nf:3 * nf]], axis=-1)


def kernel(x, c, ctx, c_ctx, w_mod, b_mod, attn_norm, w_in, q_a_norm, w_q_b, kv_a_norm, w_kv_b, conv_w,
           o_norm_mla, o_norm_conv, w_out, ffn_norm, w_router, router_bias, w_exp_gate, w_exp_up,
           w_exp_down, w_sh_gate, w_sh_up, w_sh_down, final_norm):
    b, s, d = x.shape
    lc = ctx.shape[1]
    assert w_mod.shape[0] == 1, "single trunk layer"
    assert s % GRID_W == 0
    tm_in, tq, tk = min(TM_IN, s), min(TQ, s), min(TK, s)
    tm_post, tm_moe = min(TM_POST, s), min(TM_MOE, s)

    wi = w_in[0]
    kr_cols = wi[:, C_KR:C_KR + D_ROPE]
    win = jnp.concatenate([wi[:, :C_KR], kr_cols, _swap_pairs(kr_cols), wi[:, C_KR + D_ROPE:]],
                          axis=-1).astype(BF16)
    wq3 = w_q_b[0].reshape(Q_RANK, N_HEADS, D_NOPE + D_ROPE)
    wq = jnp.concatenate([wq3, _swap_pairs(wq3[..., D_NOPE:])], axis=-1)
    wq = wq.reshape(Q_RANK, N_HEADS * D_QK_PAD).astype(BF16)
    wkv3 = w_kv_b[0].reshape(KV_RANK, N_HEADS, D_NOPE + D_V)
    wkv = jnp.concatenate([wkv3[..., :D_NOPE].reshape(KV_RANK, -1),
                           wkv3[..., D_NOPE:].reshape(KV_RANK, -1)], axis=-1).astype(BF16)
    wr = jnp.pad(w_router[0], ((0, 0), (0, LANES - N_EXPERTS)))
    wrh, wrl = _split_bf16(wr)
    wgu = jnp.concatenate([w_exp_gate[0], w_exp_up[0]], axis=-1).astype(BF16)
    wd = w_exp_down[0].astype(BF16)
    wsgu = jnp.concatenate([w_sh_gate[0], w_sh_up[0]], axis=-1).astype(BF16)
    wsd = w_sh_down[0].astype(BF16)
    row = lambda v: v.reshape(1, -1)

    n_rows = -(-(b + 1) // 8) * 8
    cc = jnp.zeros((n_rows, d), F32).at[:b].set(c).at[b].set(c_ctx)
    mods = _mods(cc, w_mod[0], row(b_mod[0])).reshape(n_rows, N_MOD, d)

    ca, sa = _rope_tables(s)
    ca_c = jnp.concatenate([jnp.ones((lc, D_ROPE), F32), jnp.zeros((lc, LANES - D_ROPE), F32)], axis=-1)
    sa_c = jnp.zeros((lc, LANES), F32)
    common = (row(attn_norm[0]), win, wq, wkv, row(q_a_norm[0]), row(kv_a_norm[0]))
    kc, vc = _inproj(ctx, mods, lambda bi: b, *common, ca_c, sa_c, latent=False, tm=min(TM_IN, lc))
    q, kx, vx, gb, u = _inproj(x, mods, lambda bi: bi, *common, ca, sa, latent=True, tm=tm_in)

    o_mla = _attention(q, kc, vc, kx, vx, tq=tq, tk=tk)

    h, xn, comb = _post(o_mla, gb, u, x, mods, conv_w[0], row(o_norm_mla[0]), row(o_norm_conv[0]),
                        w_out[0].astype(BF16), row(ffn_norm[0]), wrh, wrl,
                        router_bias[0].reshape(-1, 1), tm=tm_post)

    return _moe(xn, h, comb, mods, wgu, wd, wsgu, wsd, row(final_norm), tm=tm_moe)
```

```python
import functools
import math

import jax
import jax.numpy as jnp
from jax import lax
from jax.experimental import pallas as pl
from jax.experimental.pallas import tpu as pltpu

F32 = jnp.float32
BF16 = jnp.bfloat16

N_HEADS = 8
D_NOPE = 128
D_ROPE = 64
D_V = 128
D_QK_PAD = 256
Q_RANK = 512
KV_RANK = 256
MLA_W = N_HEADS * D_V
GRID_W = 64
ROPE_BASE = 10000.0
ATTN_SCALE = (D_NOPE + D_ROPE) ** -0.5
LOG2E = math.log2(math.e)
N_EXPERTS = 32
N_GROUPS = 4
GROUP_SIZE = N_EXPERTS // N_GROUPS
TOPK_GROUPS = 2
TOP_K = 4
ROUTED_SCALE = 2.5
N_MOD = 6
EPS = 1e-6

C_Q = 0
C_KV = Q_RANK
C_KR = C_KV + KV_RANK
C_GB = C_KR + 128

LANES = 128
BF16_SUBLANES = 16
VMEM_LIMIT = 56 * 1024 * 1024

TM_IN = 512
TM_POST = 512
TM_DISPATCH = 512
TM_EXPERT = 512
TM_COMBINE = 256
TN_MODS = 1024
DMA_UNROLL = 8


def _rms(x, g):
    return x * lax.rsqrt(jnp.mean(x * x, axis=-1, keepdims=True) + EPS) * g


def _silu(x):
    return x / (1.0 + jnp.exp(-x))


def _split_bf16(x):
    hi = x.astype(BF16)
    lo = (x - hi.astype(F32)).astype(BF16)
    return hi, lo


def _dot(a, b):
    return jnp.dot(a, b, preferred_element_type=F32)


def _mods_kernel(a_ref, w_ref, b_ref, o_ref):
    a = _silu(a_ref[...])
    ah, al = _split_bf16(a)
    wh, wl = _split_bf16(w_ref[...])
    o_ref[...] = _dot(ah, wh) + _dot(al, wh) + _dot(ah, wl) + b_ref[...]


def _mods(cc, w_mod, b_mod):
    rows, d = cc.shape
    n = w_mod.shape[1]
    return pl.pallas_call(
        _mods_kernel,
        grid=(n // TN_MODS,),
        in_specs=[pl.BlockSpec((rows, d), lambda j: (0, 0)),
                  pl.BlockSpec((d, TN_MODS), lambda j: (0, j)),
                  pl.BlockSpec((1, TN_MODS), lambda j: (0, j))],
        out_specs=pl.BlockSpec((rows, TN_MODS), lambda j: (0, j)),
        out_shape=jax.ShapeDtypeStruct((rows, n), F32),
        compiler_params=pltpu.CompilerParams(dimension_semantics=("arbitrary",),
                                             vmem_limit_bytes=VMEM_LIMIT),
        name="mods",
    )(cc, w_mod, b_mod)


def _inproj_kernel(x_ref, mods_ref, g_ref, win_ref, wqt_ref, wk_ref, wvt_ref, qg_ref, kvg_ref,
                   ca_ref, sa_ref, ct_ref, st_ref, *out_refs, latent):
    x = x_ref[0]
    shift = mods_ref[0, 0:1, :]
    scale = mods_ref[0, 1:2, :]
    nx = (_rms(x, g_ref[...]) * (1.0 + scale) + shift).astype(BF16)

    if latent:
        qt_ref, k_ref, vt_ref, gb_ref, u_ref = out_refs
        q_a = _dot(nx, win_ref[:, C_Q:C_KV])
        qnt = _rms(q_a, qg_ref[...]).T.astype(BF16)
        qt = _dot(wqt_ref[...], qnt) * (ATTN_SCALE * LOG2E)
        ct = ct_ref[...]
        st = st_ref[...]
        for h in range(N_HEADS):
            r0 = h * D_QK_PAD
            r1 = r0 + D_NOPE
            r2 = r1 + D_ROPE
            qt_ref[0, 0, r0:r1, :] = qt[r0:r1].astype(BF16)
            qt_ref[0, 0, r1:r2, :] = (qt[r1:r2] * ct + qt[r2:r0 + D_QK_PAD] * st).astype(BF16)
            qt_ref[0, 0, r2:r0 + D_QK_PAD, :] = jnp.zeros((D_ROPE, qt.shape[1]), BF16)
        gb_ref[0] = _dot(nx, win_ref[:, C_GB:C_GB + MLA_W]).astype(BF16)
        g_c = _dot(nx, win_ref[:, C_GB + MLA_W:C_GB + 2 * MLA_W])
        hh = _dot(nx, win_ref[:, C_GB + 2 * MLA_W:C_GB + 3 * MLA_W])
        u_ref[0] = (g_c * hh).astype(BF16)
    else:
        k_ref, vt_ref = out_refs

    kv_a = _dot(nx, win_ref[:, C_KV:C_KR])
    kvn = _rms(kv_a, kvg_ref[...])
    kn = _dot(kvn.astype(BF16), wk_ref[...])
    vt_ref[0, 0] = _dot(wvt_ref[...], kvn.T.astype(BF16)).astype(BF16)
    blk = _dot(nx, win_ref[:, C_KR:C_GB])
    kr = (blk * ca_ref[...] + pltpu.roll(blk, D_ROPE, axis=1) * sa_ref[...]).astype(BF16)
    for h in range(N_HEADS):
        c0 = h * D_QK_PAD
        k_ref[0, :, c0:c0 + D_NOPE] = kn[:, h * D_NOPE:(h + 1) * D_NOPE].astype(BF16)
        k_ref[0, :, c0 + D_NOPE:c0 + D_QK_PAD] = kr


def _inproj(x, mods, mod_row, g, win, wqt, wk, wvt, qg, kvg, ca, sa, ct, st, *, latent, tm):
    b, s, d = x.shape
    nt = s // tm
    const = lambda a: pl.BlockSpec(a.shape, lambda bi, i: (0,) * a.ndim, pipeline_mode=pl.Buffered(1))
    tok = lambda w: pl.BlockSpec((1, tm, w), lambda bi, i: (bi, i, 0))
    tok_t = lambda w: pl.BlockSpec((1, 1, w, tm), lambda bi, i: (bi, i, 0, 0))
    in_specs = [tok(d),
                pl.BlockSpec((1, N_MOD, d), lambda bi, i: (mod_row(bi), 0, 0)),
                const(g), const(win), const(wqt), const(wk), const(wvt), const(qg), const(kvg),
                pl.BlockSpec((tm, LANES), lambda bi, i: (i, 0)),
                pl.BlockSpec((tm, LANES), lambda bi, i: (i, 0)),
                pl.BlockSpec((D_ROPE, tm), lambda bi, i: (0, i)),
                pl.BlockSpec((D_ROPE, tm), lambda bi, i: (0, i))]
    kq = N_HEADS * D_QK_PAD
    row_major = lambda w: (tok(w), jax.ShapeDtypeStruct((b, s, w), BF16))
    col_major = lambda w: (tok_t(w), jax.ShapeDtypeStruct((b, nt, w, tm), BF16))
    outs = [row_major(kq), col_major(MLA_W)]
    if latent:
        outs = [col_major(kq)] + outs + [row_major(MLA_W), row_major(MLA_W)]
    return pl.pallas_call(
        functools.partial(_inproj_kernel, latent=latent),
        grid=(b, nt),
        in_specs=in_specs,
        out_specs=[o[0] for o in outs],
        out_shape=[o[1] for o in outs],
        compiler_params=pltpu.CompilerParams(dimension_semantics=("parallel", "arbitrary"),
                                             vmem_limit_bytes=VMEM_LIMIT),
        name="inproj_latent" if latent else "inproj_ctx",
    )(x, mods, g, win, wqt, wk, wvt, qg, kvg, ca, sa, ct, st)


def _attn_kernel(qt_ref, kc_ref, vct_ref, kx_ref, vxt_ref, o_ref, s_ref):
    qt = qt_ref[0, 0]
    tq = qt.shape[1]
    n = vxt_ref.shape[1]
    tk = kx_ref.shape[1] // n

    def scores(k):
        s = _dot(k, qt)
        return s, jnp.max(s, axis=0, keepdims=True)

    def keys(j):
        return kx_ref[0, pl.ds(pl.multiple_of(j * tk, tk), tk), :]

    def absorb(s, mx, vt, carry):
        m, l, acc = carry
        m_new = jnp.maximum(m, mx)
        alpha = jnp.exp2(m - m_new)
        p = jnp.exp2(s - m_new)
        l = alpha * l + jnp.sum(p, axis=0, keepdims=True)
        acc = alpha * acc + _dot(vt, p.astype(BF16))
        return m_new, l, acc

    init = (jnp.full((1, tq), -jnp.inf, F32), jnp.zeros((1, tq), F32), jnp.zeros((D_V, tq), F32))
    s, mx = scores(kc_ref[0])
    carry = absorb(s, mx, vct_ref[0, 0], init)

    s, mx = scores(keys(0))
    s_ref[0] = s

    def pair(j, carry, mx, last):
        s_b, mx_b = scores(keys(j + 1))
        s_ref[1] = s_b
        carry = absorb(s_ref[0], mx, vxt_ref[0, j], carry)
        if not last:
            s_a, mx = scores(keys(j + 2))
            s_ref[0] = s_a
        carry = absorb(s_ref[1], mx_b, vxt_ref[0, j + 1], carry)
        return carry, mx

    def body(i, state):
        return pair(2 * i, *state, last=False)

    carry, mx = lax.fori_loop(0, n // 2 - 1, body, (carry, mx))
    (_, l, acc), _ = pair(n - 2, carry, mx, last=True)
    o_ref[0] = (acc / l).T.astype(o_ref.dtype)


def _attention(qt, kc, vct, kx, vxt):
    b, nt, _, tq = qt.shape
    s = kx.shape[1]
    lc = kc.shape[1]
    assert nt % 2 == 0, "latent key chunks are processed in pairs"
    return pl.pallas_call(
        _attn_kernel,
        grid=(b, N_HEADS, nt),
        in_specs=[pl.BlockSpec((1, 1, D_QK_PAD, tq), lambda bi, h, i: (bi, i, h, 0)),
                  pl.BlockSpec((1, lc, D_QK_PAD), lambda bi, h, i: (bi, 0, h)),
                  pl.BlockSpec((1, 1, D_V, lc), lambda bi, h, i: (bi, 0, h, 0)),
                  pl.BlockSpec((1, s, D_QK_PAD), lambda bi, h, i: (bi, 0, h)),
                  pl.BlockSpec((1, nt, D_V, tq), lambda bi, h, i: (bi, 0, h, 0))],
        out_specs=pl.BlockSpec((1, tq, D_V), lambda bi, h, i: (bi, i, h)),
        out_shape=jax.ShapeDtypeStruct((b, s, MLA_W), BF16),
        scratch_shapes=[pltpu.VMEM((2, s // nt, tq), F32)],
        compiler_params=pltpu.CompilerParams(
            dimension_semantics=("parallel", "parallel", "arbitrary"), vmem_limit_bytes=VMEM_LIMIT),
        name="attn",
    )(qt, kc, vct, kx, vxt)


def _route(logits_t, bias):
    e, tm = logits_t.shape
    neg = -jnp.inf
    s = 1.0 / (1.0 + jnp.exp(-logits_t))
    sel = s + bias
    io_g = lax.broadcasted_iota(jnp.int32, (GROUP_SIZE, tm), 0)
    gs = []
    for g in range(N_GROUPS):
        blk = sel[g * GROUP_SIZE:(g + 1) * GROUP_SIZE]
        m1 = jnp.max(blk, axis=0, keepdims=True)
        i1 = jnp.min(jnp.where(blk == m1, io_g, GROUP_SIZE), axis=0, keepdims=True)
        m2 = jnp.max(jnp.where(io_g == i1, neg, blk), axis=0, keepdims=True)
        gs.append(m1 + m2)
    masks = []
    for g in range(N_GROUPS):
        rank = jnp.zeros((1, tm), jnp.int32)
        for o in range(N_GROUPS):
            if o == g:
                continue
            ahead = (gs[o] > gs[g]) | (gs[o] == gs[g]) if o < g else (gs[o] > gs[g])
            rank = rank + ahead.astype(jnp.int32)
        masks.append(jnp.broadcast_to(rank < TOPK_GROUPS, (GROUP_SIZE, tm)))
    emask = jnp.concatenate(masks, axis=0)
    cur = jnp.where(emask, sel, neg)
    io_e = lax.broadcasted_iota(jnp.int32, (e, tm), 0)
    chosen = jnp.zeros((e, tm), jnp.bool_)
    slots = []
    for _ in range(TOP_K):
        m = jnp.max(cur, axis=0, keepdims=True)
        i = jnp.min(jnp.where(cur == m, io_e, e), axis=0, keepdims=True)
        hit = io_e == i
        chosen = chosen | hit
        cur = jnp.where(hit, neg, cur)
        slots.append((hit, i))
    wsum = jnp.sum(jnp.where(chosen, s, 0.0), axis=0, keepdims=True)
    return chosen, jnp.where(chosen, s / wsum * ROUTED_SCALE, 0.0), slots


def _post_kernel(om_ref, gb_ref, u_ref, up_ref, un_ref, x_ref, mods_ref, cw_ref, g1_ref, g2_ref,
                 wout_ref, gf_ref, wrh_ref, wrl_ref, rb_ref, tri_ref,
                 h_ref, xn_ref, ri_ref, rw_ref, cnt_ref, run_ref):
    first = (pl.program_id(0) == 0) & (pl.program_id(1) == 0)

    @pl.when(first)
    def _():
        run_ref[...] = jnp.zeros_like(run_ref)

    i = pl.program_id(1)
    nt = pl.num_programs(1)
    tm = u_ref.shape[1]
    u = u_ref[0].astype(F32)
    prev = jnp.where(i > 0, up_ref[0].astype(F32)[BF16_SUBLANES - 1:BF16_SUBLANES], 0.0)
    nxt = jnp.where(i < nt - 1, un_ref[0].astype(F32)[0:1], 0.0)
    row = lax.broadcasted_iota(jnp.int32, u.shape, 0)
    u_m1 = jnp.where(row == 0, prev, pltpu.roll(u, 1, axis=0))
    u_p1 = jnp.where(row == tm - 1, nxt, pltpu.roll(u, tm - 1, axis=0))
    cw = cw_ref[...]
    y = gb_ref[0].astype(F32) * (cw[0:1] * u_m1 + cw[1:2] * u + cw[2:3] * u_p1)
    o = jnp.concatenate([_rms(om_ref[0].astype(F32), g1_ref[...]).astype(BF16),
                         _rms(y, g2_ref[...]).astype(BF16)], axis=-1)
    ga = mods_ref[0, 2:3, :]
    sf = mods_ref[0, 3:4, :]
    scf = mods_ref[0, 4:5, :]
    h = x_ref[0] + ga * _dot(o, wout_ref[...])
    h_ref[0] = h
    xn = _rms(h, gf_ref[...]) * (1.0 + scf) + sf
    xh, xl = _split_bf16(xn)
    xn_ref[...] = xn
    wrh = wrh_ref[...]
    logits = _dot(xh, wrh) + _dot(xl, wrh) + _dot(xh, wrl_ref[...])
    chosen, comb_t, slots = _route(logits.T[:N_EXPERTS], rb_ref[...])
    sel = jnp.where(chosen, 1.0, 0.0)
    before = _dot(sel.astype(BF16), tri_ref[...]) + run_ref[...]
    run_ref[...] += jnp.sum(sel, axis=1, keepdims=True)
    cnt_ref[...] = jnp.broadcast_to(run_ref[...], cnt_ref.shape).astype(jnp.int32)
    pick = lambda hit, v: jnp.sum(jnp.where(hit, v, 0.0), axis=0, keepdims=True)
    ri_ref[...] = jnp.concatenate(
        [idx for _, idx in slots] + [pick(hit, before).astype(jnp.int32) for hit, _ in slots], axis=0)
    rw_ref[...] = jnp.concatenate(
        [pick(hit, comb_t) for hit, _ in slots] + [jnp.zeros((LANES - TOP_K, tm), F32)], axis=0).T


def _post(o_mla, gb, u, x, mods, conv_w, g1, g2, wout, gf, wrh, wrl, rb, *, tm):
    b, s, d = x.shape
    nt = s // tm
    hb = tm // BF16_SUBLANES
    last = s // BF16_SUBLANES - 1
    tri = (lax.broadcasted_iota(jnp.int32, (tm, tm), 0)
           < lax.broadcasted_iota(jnp.int32, (tm, tm), 1)).astype(BF16)
    const = lambda a: pl.BlockSpec(a.shape, lambda bi, i: (0,) * a.ndim, pipeline_mode=pl.Buffered(1))
    tok = lambda w: pl.BlockSpec((1, tm, w), lambda bi, i: (bi, i, 0))
    flat = lambda w: pl.BlockSpec((tm, w), lambda bi, i: (bi * nt + i, 0))
    in_specs = [tok(MLA_W), tok(MLA_W), tok(MLA_W),
                pl.BlockSpec((1, BF16_SUBLANES, MLA_W), lambda bi, i: (bi, jnp.maximum(i * hb - 1, 0), 0)),
                pl.BlockSpec((1, BF16_SUBLANES, MLA_W), lambda bi, i: (bi, jnp.minimum((i + 1) * hb, last), 0)),
                tok(d),
                pl.BlockSpec((1, N_MOD, d), lambda bi, i: (bi, 0, 0)),
                const(conv_w), const(g1), const(g2), const(wout), const(gf), const(wrh), const(wrl),
                const(rb), const(tri)]
    t = b * s
    return pl.pallas_call(
        _post_kernel,
        grid=(b, nt),
        in_specs=in_specs,
        out_specs=[tok(d), flat(d),
                   pl.BlockSpec((2 * TOP_K, tm), lambda bi, i: (0, bi * nt + i)),
                   flat(LANES),
                   pl.BlockSpec((N_EXPERTS, LANES), lambda bi, i: (0, 0))],
        out_shape=[jax.ShapeDtypeStruct((b, s, d), F32),
                   jax.ShapeDtypeStruct((t, d), F32),
                   jax.ShapeDtypeStruct((2 * TOP_K, t), jnp.int32),
                   jax.ShapeDtypeStruct((t, LANES), F32),
                   jax.ShapeDtypeStruct((N_EXPERTS, LANES), jnp.int32)],
        scratch_shapes=[pltpu.VMEM((N_EXPERTS, 1), F32)],
        compiler_params=pltpu.CompilerParams(dimension_semantics=("arbitrary", "arbitrary"),
                                             vmem_limit_bytes=VMEM_LIMIT),
        name="post",
    )(o_mla, gb, u, u, u, x, mods, conv_w, g1, g2, wout, gf, wrh, wrl, rb, tri)


def _sorted_row(base_ref, ri_ref, k, t):
    return base_ref[ri_ref[k, t]] + ri_ref[TOP_K + k, t]


def _dispatch_kernel(base_ref, zt_ref, ri_ref, xn_ref, xs_ref, zero_ref, sem):
    tm = xn_ref.shape[0]
    tz = zero_ref.shape[0]

    @pl.when(pl.program_id(0) == 0)
    def _():
        zero_ref[...] = jnp.zeros_like(zero_ref)
        n_fill = zt_ref.shape[0]
        fill = lambda j: pltpu.make_async_copy(
            zero_ref, xs_ref.at[pl.ds(jnp.maximum(zt_ref[j], 0) * tz, tz)], sem)
        for j in range(n_fill):
            pl.when(zt_ref[j] >= 0)(fill(j).start)
        for j in range(n_fill):
            pl.when(zt_ref[j] >= 0)(fill(j).wait)

    def issue(t, carry):
        for k in range(TOP_K):
            dst = _sorted_row(base_ref, ri_ref, k, t)
            pltpu.make_async_copy(xn_ref.at[pl.ds(t, 1)], xs_ref.at[pl.ds(dst, 1)], sem).start()
        return carry

    lax.fori_loop(0, tm, issue, 0, unroll=DMA_UNROLL)
    for _ in range(TOP_K):
        pltpu.make_async_copy(xn_ref, xs_ref.at[pl.ds(0, tm)], sem).wait()


def _dispatch(base, zt, ri, xn, *, n_rows, tm, tile):
    t, w = xn.shape
    return pl.pallas_call(
        _dispatch_kernel,
        grid_spec=pltpu.PrefetchScalarGridSpec(
            num_scalar_prefetch=2,
            grid=(t // tm,),
            in_specs=[pl.BlockSpec((2 * TOP_K, tm), lambda i, base, zt: (0, i), memory_space=pltpu.SMEM),
                      pl.BlockSpec((tm, w), lambda i, base, zt: (i, 0))],
            out_specs=pl.BlockSpec(memory_space=pl.ANY),
            scratch_shapes=[pltpu.VMEM((tile, w), xn.dtype), pltpu.SemaphoreType.DMA(())]),
        out_shape=jax.ShapeDtypeStruct((n_rows, w), xn.dtype),
        compiler_params=pltpu.CompilerParams(dimension_semantics=("arbitrary",),
                                             vmem_limit_bytes=VMEM_LIMIT),
        name="dispatch",
    )(base, zt, ri, xn)


def _expert_kernel(te_ref, tv_ref, xs_ref, wgu_ref, wd_ref, ys_ref):
    nv = tv_ref[pl.program_id(0)]
    ff = wd_ref.shape[1]

    @pl.when(nv > 0)
    def _():
        rows = lax.broadcasted_iota(jnp.int32, xs_ref.shape, 0)
        x = jnp.where(rows < nv, xs_ref[...], 0.0).astype(BF16)
        gu = _dot(x, wgu_ref[0])
        a = (_silu(gu[:, :ff]) * gu[:, ff:]).astype(BF16)
        ys_ref[...] = _dot(a, wd_ref[0])

    @pl.when(nv == 0)
    def _():
        ys_ref[...] = jnp.zeros_like(ys_ref)


def _experts(te, tv, xs, wgu, wd, *, tm):
    n_rows, w = xs.shape
    d = wd.shape[2]
    return pl.pallas_call(
        _expert_kernel,
        grid_spec=pltpu.PrefetchScalarGridSpec(
            num_scalar_prefetch=2,
            grid=(n_rows // tm,),
            in_specs=[pl.BlockSpec((tm, w), lambda i, te, tv: (i, 0)),
                      pl.BlockSpec((1,) + wgu.shape[1:], lambda i, te, tv: (te[i], 0, 0)),
                      pl.BlockSpec((1,) + wd.shape[1:], lambda i, te, tv: (te[i], 0, 0))],
            out_specs=pl.BlockSpec((tm, d), lambda i, te, tv: (i, 0))),
        out_shape=jax.ShapeDtypeStruct((n_rows, d), F32),
        compiler_params=pltpu.CompilerParams(dimension_semantics=("arbitrary",),
                                             vmem_limit_bytes=VMEM_LIMIT),
        name="experts",
    )(te, tv, xs, wgu, wd)


def _combine_kernel(base_ref, ri_ref, rw_ref, xn_ref, h_ref, mods_ref, wsgu_ref, wsd_ref, gfin_ref,
                    ys_ref, o_ref, buf_ref, sem):
    tm = xn_ref.shape[0]

    def issue(t, carry):
        for k in range(TOP_K):
            src = _sorted_row(base_ref, ri_ref, k, t)
            pltpu.make_async_copy(ys_ref.at[pl.ds(src, 1)], buf_ref.at[k, pl.ds(t, 1)], sem).start()
        return carry

    lax.fori_loop(0, tm, issue, 0, unroll=DMA_UNROLL)

    sgu = _dot(xn_ref[...].astype(BF16), wsgu_ref[...])
    sf = sgu.shape[1] // 2
    y = _dot((_silu(sgu[:, :sf]) * sgu[:, sf:]).astype(BF16), wsd_ref[...])

    for k in range(TOP_K):
        pltpu.make_async_copy(ys_ref.at[pl.ds(0, tm)], buf_ref.at[k], sem).wait()
    rw = rw_ref[...]
    for k in range(TOP_K):
        y = y + rw[:, k:k + 1] * buf_ref[k]
    gate = mods_ref[0, 5:6, :]
    o_ref[0] = _rms(h_ref[0] + gate * y, gfin_ref[...])


def _combine(base, ri, rw, xn, h, mods, wsgu, wsd, gfin, ys, *, tm):
    b, s, d = h.shape
    nt = s // tm
    const = lambda a: pl.BlockSpec(a.shape, lambda bi, i, base: (0,) * a.ndim,
                                   pipeline_mode=pl.Buffered(1))
    flat = lambda w: pl.BlockSpec((tm, w), lambda bi, i, base: (bi * nt + i, 0))
    return pl.pallas_call(
        _combine_kernel,
        grid_spec=pltpu.PrefetchScalarGridSpec(
            num_scalar_prefetch=1,
            grid=(b, nt),
            in_specs=[pl.BlockSpec((2 * TOP_K, tm), lambda bi, i, base: (0, bi * nt + i),
                                   memory_space=pltpu.SMEM),
                      flat(LANES), flat(xn.shape[1]),
                      pl.BlockSpec((1, tm, d), lambda bi, i, base: (bi, i, 0)),
                      pl.BlockSpec((1, N_MOD, d), lambda bi, i, base: (bi, 0, 0)),
                      const(wsgu), const(wsd), const(gfin),
                      pl.BlockSpec(memory_space=pl.ANY)],
            out_specs=pl.BlockSpec((1, tm, d), lambda bi, i, base: (bi, i, 0)),
            scratch_shapes=[pltpu.VMEM((TOP_K, tm, d), F32), pltpu.SemaphoreType.DMA(())]),
        out_shape=jax.ShapeDtypeStruct((b, s, d), F32),
        compiler_params=pltpu.CompilerParams(dimension_semantics=("arbitrary", "arbitrary"),
                                             vmem_limit_bytes=VMEM_LIMIT),
        name="combine",
    )(base, ri, rw, xn, h, mods, wsgu, wsd, gfin, ys)


def _rope_tables(s):
    nf = D_ROPE // 4
    pos = jnp.arange(s, dtype=jnp.int32)
    row = (pos // GRID_W).astype(F32)
    col = (pos % GRID_W).astype(F32)
    inv = ROPE_BASE ** (-jnp.arange(nf, dtype=F32) / nf)
    ar, ac = row[:, None] * inv, col[:, None] * inv
    c64 = jnp.concatenate([jnp.cos(ar), jnp.cos(ar), jnp.cos(ac), jnp.cos(ac)], axis=-1)
    s64 = jnp.concatenate([-jnp.sin(ar), jnp.sin(ar), -jnp.sin(ac), jnp.sin(ac)], axis=-1)
    return c64, s64


def _swap_pairs(w):
    nf = D_ROPE // 4
    return jnp.concatenate([w[..., nf:2 * nf], w[..., :nf], w[..., 3 * nf:], w[..., 2 * nf:3 * nf]], axis=-1)


def kernel(x, c, ctx, c_ctx, w_mod, b_mod, attn_norm, w_in, q_a_norm, w_q_b, kv_a_norm, w_kv_b, conv_w,
           o_norm_mla, o_norm_conv, w_out, ffn_norm, w_router, router_bias, w_exp_gate, w_exp_up,
           w_exp_down, w_sh_gate, w_sh_up, w_sh_down, final_norm):
    b, s, d = x.shape
    lc = ctx.shape[1]
    assert w_mod.shape[0] == 1, "single trunk layer"
    assert s % GRID_W == 0
    assert lc <= TM_IN, "context keys are processed as a single attention chunk"
    tm_in, tm_post = min(TM_IN, s), min(TM_POST, s)

    wi = w_in[0]
    kr_cols = wi[:, C_KR:C_KR + D_ROPE]
    win = jnp.concatenate([wi[:, :C_KR], kr_cols, _swap_pairs(kr_cols), wi[:, C_KR + D_ROPE:]],
                          axis=-1).astype(BF16)
    wq3 = w_q_b[0].reshape(Q_RANK, N_HEADS, D_NOPE + D_ROPE)
    wq = jnp.concatenate([wq3, _swap_pairs(wq3[..., D_NOPE:])], axis=-1)
    wqt = wq.reshape(Q_RANK, N_HEADS * D_QK_PAD).T.astype(BF16)
    wkv3 = w_kv_b[0].reshape(KV_RANK, N_HEADS, D_NOPE + D_V)
    wk = wkv3[..., :D_NOPE].reshape(KV_RANK, MLA_W).astype(BF16)
    wvt = wkv3[..., D_NOPE:].reshape(KV_RANK, MLA_W).T.astype(BF16)
    wr = jnp.pad(w_router[0], ((0, 0), (0, LANES - N_EXPERTS)))
    wrh, wrl = _split_bf16(wr)
    wgu = jnp.concatenate([w_exp_gate[0], w_exp_up[0]], axis=-1).astype(BF16)
    wd = w_exp_down[0].astype(BF16)
    wsgu = jnp.concatenate([w_sh_gate[0], w_sh_up[0]], axis=-1).astype(BF16)
    wsd = w_sh_down[0].astype(BF16)
    row = lambda v: v.reshape(1, -1)

    n_rows = -(-(b + 1) // 8) * 8
    cc = jnp.zeros((n_rows, d), F32).at[:b].set(c).at[b].set(c_ctx)
    mods = _mods(cc, w_mod[0], row(b_mod[0])).reshape(n_rows, N_MOD, d)

    c64, s64 = _rope_tables(s)
    pad = lambda t: jnp.pad(t, ((0, 0), (0, LANES - D_ROPE)))
    ones_c, zeros_c = jnp.ones((lc, D_ROPE), F32), jnp.zeros((lc, D_ROPE), F32)
    common = (row(attn_norm[0]), win, wqt, wk, wvt, row(q_a_norm[0]), row(kv_a_norm[0]))
    kc, vct = _inproj(ctx, mods, lambda bi: b, *common, pad(ones_c), pad(zeros_c), ones_c.T, zeros_c.T,
                      latent=False, tm=min(TM_IN, lc))
    qt, kx, vxt, gb, u = _inproj(x, mods, lambda bi: bi, *common, pad(c64), pad(s64), c64.T, s64.T,
                                 latent=True, tm=tm_in)

    o_mla = _attention(qt, kc, vct, kx, vxt)

    h, xn, ri, rw, cnt = _post(o_mla, gb, u, x, mods, conv_w[0], row(o_norm_mla[0]), row(o_norm_conv[0]),
                               w_out[0].astype(BF16), row(ffn_norm[0]), wrh, wrl,
                               router_bias[0].reshape(-1, 1), tm=tm_post)

    tmx = TM_EXPERT
    n_tiles = pl.cdiv(b * s * TOP_K, tmx) + N_EXPERTS
    counts = cnt[:, 0]
    tiles_per = (counts + tmx - 1) // tmx
    tile_end = jnp.cumsum(tiles_per)
    tile_start = tile_end - tiles_per
    base = (tile_start * tmx).astype(jnp.int32)
    tile_ids = jnp.arange(n_tiles, dtype=jnp.int32)
    te = jnp.minimum(jnp.sum(tile_end[None, :] <= tile_ids[:, None], axis=1), N_EXPERTS - 1)
    own = te[:, None] == jnp.arange(N_EXPERTS)[None, :]
    left = jnp.sum(jnp.where(own, counts[None, :] - (tile_ids[:, None] - tile_start[None, :]) * tmx, 0),
                   axis=1)
    tv = jnp.where(tile_ids < tile_end[-1], jnp.clip(left, 0, tmx), 0).astype(jnp.int32)
    tail = tile_end[-1] + jnp.arange(N_EXPERTS)
    zt = jnp.concatenate([jnp.where(tiles_per > 0, tile_end - 1, -1),
                          jnp.where(tail < n_tiles, tail, -1)]).astype(jnp.int32)

    xs = _dispatch(base, zt, ri, xn, n_rows=n_tiles * tmx, tm=min(TM_DISPATCH, s), tile=tmx)
    ys = _experts(te.astype(jnp.int32), tv, xs, wgu, wd, tm=tmx)
    return _combine(base, ri, rw, xn, h, mods, wsgu, wsd, row(final_norm), ys, tm=min(TM_COMBINE, s))
```

```python
import functools
import math

import jax
import jax.numpy as jnp
from jax import lax
from jax.experimental import pallas as pl
from jax.experimental.pallas import tpu as pltpu

F32 = jnp.float32
BF16 = jnp.bfloat16

N_HEADS = 8
D_NOPE = 128
D_ROPE = 64
D_V = 128
D_QK_PAD = 256
Q_RANK = 512
KV_RANK = 256
MLA_W = N_HEADS * D_V
GRID_W = 64
ROPE_BASE = 10000.0
ATTN_SCALE = (D_NOPE + D_ROPE) ** -0.5
LOG2E = math.log2(math.e)
N_EXPERTS = 32
N_GROUPS = 4
GROUP_SIZE = N_EXPERTS // N_GROUPS
TOPK_GROUPS = 2
TOP_K = 4
ROUTED_SCALE = 2.5
N_MOD = 6
EPS = 1e-6

C_Q = 0
C_KV = Q_RANK
C_KR = C_KV + KV_RANK
C_GB = C_KR + 128

LANES = 128
BF16_SUBLANES = 16
VMEM_LIMIT = 56 * 1024 * 1024

TM_IN = 512
ATTN_Q_TILES = 4
TM_POST = 512
TM_POSITIONS = 4096
TM_DISPATCH = 512
TM_EXPERT = 512
TM_COMBINE = 256
TN_MODS = 1024
DMA_UNROLL = 8


def _rms(x, g):
    return x * lax.rsqrt(jnp.mean(x * x, axis=-1, keepdims=True) + EPS) * g


def _silu(x):
    return x / (1.0 + jnp.exp(-x))


def _split_bf16(x):
    hi = x.astype(BF16)
    lo = (x - hi.astype(F32)).astype(BF16)
    return hi, lo


def _dot(a, b):
    return jnp.dot(a, b, preferred_element_type=F32)


def _mods_kernel(a_ref, w_ref, b_ref, o_ref):
    a = _silu(a_ref[...])
    ah, al = _split_bf16(a)
    wh, wl = _split_bf16(w_ref[...])
    o_ref[...] = _dot(ah, wh) + _dot(al, wh) + _dot(ah, wl) + b_ref[...]


def _mods(cc, w_mod, b_mod):
    rows, d = cc.shape
    n = w_mod.shape[1]
    return pl.pallas_call(
        _mods_kernel,
        grid=(n // TN_MODS,),
        in_specs=[pl.BlockSpec((rows, d), lambda j: (0, 0)),
                  pl.BlockSpec((d, TN_MODS), lambda j: (0, j)),
                  pl.BlockSpec((1, TN_MODS), lambda j: (0, j))],
        out_specs=pl.BlockSpec((rows, TN_MODS), lambda j: (0, j)),
        out_shape=jax.ShapeDtypeStruct((rows, n), F32),
        compiler_params=pltpu.CompilerParams(dimension_semantics=("arbitrary",),
                                             vmem_limit_bytes=VMEM_LIMIT),
        name="mods",
    )(cc, w_mod, b_mod)


def _inproj_kernel(x_ref, mods_ref, g_ref, win_ref, wqt_ref, wk_ref, wvt_ref, qg_ref, kvg_ref,
                   ca_ref, sa_ref, ct_ref, st_ref, *out_refs, latent):
    x = x_ref[0]
    shift = mods_ref[0, 0:1, :]
    scale = mods_ref[0, 1:2, :]
    nx = (_rms(x, g_ref[...]) * (1.0 + scale) + shift).astype(BF16)

    if latent:
        qt_ref, k_ref, vt_ref, gb_ref, u_ref = out_refs
        q_a = _dot(nx, win_ref[:, C_Q:C_KV])
        qnt = _rms(q_a, qg_ref[...]).T.astype(BF16)
        qt = _dot(wqt_ref[...], qnt) * (ATTN_SCALE * LOG2E)
        ct = ct_ref[...]
        st = st_ref[...]
        for h in range(N_HEADS):
            r0 = h * D_QK_PAD
            r1 = r0 + D_NOPE
            r2 = r1 + D_ROPE
            qt_ref[0, 0, r0:r1, :] = qt[r0:r1].astype(BF16)
            qt_ref[0, 0, r1:r2, :] = (qt[r1:r2] * ct + qt[r2:r0 + D_QK_PAD] * st).astype(BF16)
            qt_ref[0, 0, r2:r0 + D_QK_PAD, :] = jnp.zeros((D_ROPE, qt.shape[1]), BF16)
        gb_ref[0] = _dot(nx, win_ref[:, C_GB:C_GB + MLA_W]).astype(BF16)
        g_c = _dot(nx, win_ref[:, C_GB + MLA_W:C_GB + 2 * MLA_W])
        hh = _dot(nx, win_ref[:, C_GB + 2 * MLA_W:C_GB + 3 * MLA_W])
        u_ref[0] = (g_c * hh).astype(BF16)
    else:
        k_ref, vt_ref = out_refs

    kv_a = _dot(nx, win_ref[:, C_KV:C_KR])
    kvn = _rms(kv_a, kvg_ref[...])
    kn = _dot(kvn.astype(BF16), wk_ref[...])
    vt_ref[0, 0] = _dot(wvt_ref[...], kvn.T.astype(BF16)).astype(BF16)
    blk = _dot(nx, win_ref[:, C_KR:C_GB])
    kr = (blk * ca_ref[...] + pltpu.roll(blk, D_ROPE, axis=1) * sa_ref[...]).astype(BF16)
    for h in range(N_HEADS):
        c0 = h * D_QK_PAD
        k_ref[0, :, c0:c0 + D_NOPE] = kn[:, h * D_NOPE:(h + 1) * D_NOPE].astype(BF16)
        k_ref[0, :, c0 + D_NOPE:c0 + D_QK_PAD] = kr


def _inproj(x, mods, mod_row, g, win, wqt, wk, wvt, qg, kvg, ca, sa, ct, st, *, latent, tm):
    b, s, d = x.shape
    nt = s // tm
    const = lambda a: pl.BlockSpec(a.shape, lambda bi, i: (0,) * a.ndim, pipeline_mode=pl.Buffered(1))
    tok = lambda w: pl.BlockSpec((1, tm, w), lambda bi, i: (bi, i, 0))
    tok_t = lambda w: pl.BlockSpec((1, 1, w, tm), lambda bi, i: (bi, i, 0, 0))
    in_specs = [tok(d),
                pl.BlockSpec((1, N_MOD, d), lambda bi, i: (mod_row(bi), 0, 0)),
                const(g), const(win), const(wqt), const(wk), const(wvt), const(qg), const(kvg),
                pl.BlockSpec((tm, LANES), lambda bi, i: (i, 0)),
                pl.BlockSpec((tm, LANES), lambda bi, i: (i, 0)),
                pl.BlockSpec((D_ROPE, tm), lambda bi, i: (0, i)),
                pl.BlockSpec((D_ROPE, tm), lambda bi, i: (0, i))]
    kq = N_HEADS * D_QK_PAD
    row_major = lambda w: (tok(w), jax.ShapeDtypeStruct((b, s, w), BF16))
    col_major = lambda w: (tok_t(w), jax.ShapeDtypeStruct((b, nt, w, tm), BF16))
    outs = [row_major(kq), col_major(MLA_W)]
    if latent:
        outs = [col_major(kq)] + outs + [row_major(MLA_W), row_major(MLA_W)]
    return pl.pallas_call(
        functools.partial(_inproj_kernel, latent=latent),
        grid=(b, nt),
        in_specs=in_specs,
        out_specs=[o[0] for o in outs],
        out_shape=[o[1] for o in outs],
        compiler_params=pltpu.CompilerParams(dimension_semantics=("parallel", "arbitrary"),
                                             vmem_limit_bytes=VMEM_LIMIT),
        name="inproj_latent" if latent else "inproj_ctx",
    )(x, mods, g, win, wqt, wk, wvt, qg, kvg, ca, sa, ct, st)


def _attn_kernel(qt_ref, kc_ref, vct_ref, kx_ref, vxt_ref, o_ref, s_ref):
    qt = jnp.concatenate([qt_ref[0, j] for j in range(qt_ref.shape[1])], axis=1)
    tq = qt.shape[1]
    nb = vxt_ref.shape[1]
    tk = kx_ref.shape[1] // nb
    lc = kc_ref.shape[1]

    def scores(k):
        s = _dot(k, qt)
        return s, jnp.max(s, axis=0, keepdims=True)

    def keys(j):
        return kx_ref[0, pl.ds(pl.multiple_of(j * tk, tk), tk), :]

    def absorb(s, mx, vt, carry):
        m, l, acc = carry
        m_new = jnp.maximum(m, mx)
        alpha = jnp.exp2(m - m_new)
        p = jnp.exp2(s - m_new)
        l = alpha * l + jnp.sum(p, axis=0, keepdims=True)
        acc = alpha * acc + _dot(vt, p.astype(BF16))
        return m_new, l, acc

    def pair(cur_rows, cur_vt, j, carry, mx):
        s_n, mx_n = scores(keys(j))
        s_ref[1] = s_n
        carry = absorb(s_ref[0, :cur_rows, :], mx, cur_vt, carry)
        s_a, mx_a = scores(keys(j + 1))
        s_ref[0] = s_a
        carry = absorb(s_ref[1], mx_n, vxt_ref[0, j], carry)
        return carry, mx_a

    init = (jnp.full((1, tq), -jnp.inf, F32), jnp.zeros((1, tq), F32), jnp.zeros((D_V, tq), F32))
    s, mx = scores(kc_ref[0])
    s_ref[0, :lc, :] = s
    state = pair(lc, vct_ref[0, 0], 0, init, mx)

    def body(i, state):
        return pair(tk, vxt_ref[0, 2 * i - 1], 2 * i, *state)

    carry, mx = lax.fori_loop(1, nb // 2, body, state)
    _, l, acc = absorb(s_ref[0], mx, vxt_ref[0, nb - 1], carry)
    o_ref[0] = (acc / l).T.astype(o_ref.dtype)


def _attention(qt, kc, vct, kx, vxt):
    b, nt, _, tk = qt.shape
    s = kx.shape[1]
    lc = kc.shape[1]
    nq = min(ATTN_Q_TILES, nt)
    tq = nq * tk
    assert nt % 2 == 0 and nt % nq == 0 and lc <= tk, \
        "latent chunks are absorbed in pairs after one context chunk"
    return pl.pallas_call(
        _attn_kernel,
        grid=(b, N_HEADS, nt // nq),
        in_specs=[pl.BlockSpec((1, nq, D_QK_PAD, tk), lambda bi, h, i: (bi, i, h, 0)),
                  pl.BlockSpec((1, lc, D_QK_PAD), lambda bi, h, i: (bi, 0, h)),
                  pl.BlockSpec((1, 1, D_V, lc), lambda bi, h, i: (bi, 0, h, 0)),
                  pl.BlockSpec((1, s, D_QK_PAD), lambda bi, h, i: (bi, 0, h)),
                  pl.BlockSpec((1, nt, D_V, tk), lambda bi, h, i: (bi, 0, h, 0))],
        out_specs=pl.BlockSpec((1, tq, D_V), lambda bi, h, i: (bi, i, h)),
        out_shape=jax.ShapeDtypeStruct((b, s, MLA_W), BF16),
        scratch_shapes=[pltpu.VMEM((2, tk, tq), F32)],
        compiler_params=pltpu.CompilerParams(
            dimension_semantics=("parallel", "parallel", "arbitrary"), vmem_limit_bytes=VMEM_LIMIT),
        name="attn",
    )(qt, kc, vct, kx, vxt)


def _route(logits_t, bias):
    e, tm = logits_t.shape
    neg = -jnp.inf
    s = 1.0 / (1.0 + jnp.exp(-logits_t))
    sel = s + bias
    io_g = lax.broadcasted_iota(jnp.int32, (GROUP_SIZE, tm), 0)
    gs = []
    for g in range(N_GROUPS):
        blk = sel[g * GROUP_SIZE:(g + 1) * GROUP_SIZE]
        m1 = jnp.max(blk, axis=0, keepdims=True)
        i1 = jnp.min(jnp.where(blk == m1, io_g, GROUP_SIZE), axis=0, keepdims=True)
        m2 = jnp.max(jnp.where(io_g == i1, neg, blk), axis=0, keepdims=True)
        gs.append(m1 + m2)
    masks = []
    for g in range(N_GROUPS):
        rank = jnp.zeros((1, tm), jnp.int32)
        for o in range(N_GROUPS):
            if o == g:
                continue
            ahead = (gs[o] > gs[g]) | (gs[o] == gs[g]) if o < g else (gs[o] > gs[g])
            rank = rank + ahead.astype(jnp.int32)
        masks.append(jnp.broadcast_to(rank < TOPK_GROUPS, (GROUP_SIZE, tm)))
    emask = jnp.concatenate(masks, axis=0)
    cur = jnp.where(emask, sel, neg)
    io_e = lax.broadcasted_iota(jnp.int32, (e, tm), 0)
    chosen = jnp.zeros((e, tm), jnp.bool_)
    slots = []
    for _ in range(TOP_K):
        m = jnp.max(cur, axis=0, keepdims=True)
        i = jnp.min(jnp.where(cur == m, io_e, e), axis=0, keepdims=True)
        hit = io_e == i
        chosen = chosen | hit
        cur = jnp.where(hit, neg, cur)
        slots.append((hit, i))
    wsum = jnp.sum(jnp.where(chosen, s, 0.0), axis=0, keepdims=True)
    return chosen, jnp.where(chosen, s / wsum * ROUTED_SCALE, 0.0), slots


def _post_kernel(om_ref, gb_ref, u_ref, up_ref, un_ref, x_ref, mods_ref, cw_ref, g1_ref, g2_ref,
                 wout_ref, gf_ref, wrh_ref, wrl_ref, rb_ref, tri_ref,
                 h_ref, xn_ref, ri_ref, rw_ref, cnt_ref, run_ref):
    first = (pl.program_id(0) == 0) & (pl.program_id(1) == 0)

    @pl.when(first)
    def _():
        run_ref[...] = jnp.zeros_like(run_ref)

    i = pl.program_id(1)
    nt = pl.num_programs(1)
    tm = u_ref.shape[1]
    u = u_ref[0].astype(F32)
    prev = jnp.where(i > 0, up_ref[0].astype(F32)[BF16_SUBLANES - 1:BF16_SUBLANES], 0.0)
    nxt = jnp.where(i < nt - 1, un_ref[0].astype(F32)[0:1], 0.0)
    row = lax.broadcasted_iota(jnp.int32, u.shape, 0)
    u_m1 = jnp.where(row == 0, prev, pltpu.roll(u, 1, axis=0))
    u_p1 = jnp.where(row == tm - 1, nxt, pltpu.roll(u, tm - 1, axis=0))
    cw = cw_ref[...]
    y = gb_ref[0].astype(F32) * (cw[0:1] * u_m1 + cw[1:2] * u + cw[2:3] * u_p1)
    o = jnp.concatenate([_rms(om_ref[0].astype(F32), g1_ref[...]).astype(BF16),
                         _rms(y, g2_ref[...]).astype(BF16)], axis=-1)
    ga = mods_ref[0, 2:3, :]
    sf = mods_ref[0, 3:4, :]
    scf = mods_ref[0, 4:5, :]
    h = x_ref[0] + ga * _dot(o, wout_ref[...])
    h_ref[0] = h
    xn = _rms(h, gf_ref[...]) * (1.0 + scf) + sf
    xh, xl = _split_bf16(xn)
    xn_ref[...] = xn
    wrh = wrh_ref[...]
    logits = _dot(xh, wrh) + _dot(xl, wrh) + _dot(xh, wrl_ref[...])
    chosen, comb_t, slots = _route(logits.T[:N_EXPERTS], rb_ref[...])
    sel = jnp.where(chosen, 1.0, 0.0)
    before = _dot(sel.astype(BF16), tri_ref[...]) + run_ref[...]
    run_ref[...] += jnp.sum(sel, axis=1, keepdims=True)
    cnt_ref[...] = jnp.broadcast_to(run_ref[...], cnt_ref.shape).astype(jnp.int32)
    pick = lambda hit, v: jnp.sum(jnp.where(hit, v, 0.0), axis=0, keepdims=True)
    ri_ref[...] = jnp.concatenate(
        [idx for _, idx in slots] + [pick(hit, before).astype(jnp.int32) for hit, _ in slots], axis=0)
    rw_ref[...] = jnp.concatenate(
        [pick(hit, comb_t) for hit, _ in slots] + [jnp.zeros((LANES - TOP_K, tm), F32)], axis=0).T


def _post(o_mla, gb, u, x, mods, conv_w, g1, g2, wout, gf, wrh, wrl, rb, *, tm):
    b, s, d = x.shape
    nt = s // tm
    hb = tm // BF16_SUBLANES
    last = s // BF16_SUBLANES - 1
    tri = (lax.broadcasted_iota(jnp.int32, (tm, tm), 0)
           < lax.broadcasted_iota(jnp.int32, (tm, tm), 1)).astype(BF16)
    const = lambda a: pl.BlockSpec(a.shape, lambda bi, i: (0,) * a.ndim, pipeline_mode=pl.Buffered(1))
    tok = lambda w: pl.BlockSpec((1, tm, w), lambda bi, i: (bi, i, 0))
    flat = lambda w: pl.BlockSpec((tm, w), lambda bi, i: (bi * nt + i, 0))
    in_specs = [tok(MLA_W), tok(MLA_W), tok(MLA_W),
                pl.BlockSpec((1, BF16_SUBLANES, MLA_W), lambda bi, i: (bi, jnp.maximum(i * hb - 1, 0), 0)),
                pl.BlockSpec((1, BF16_SUBLANES, MLA_W), lambda bi, i: (bi, jnp.minimum((i + 1) * hb, last), 0)),
                tok(d),
                pl.BlockSpec((1, N_MOD, d), lambda bi, i: (bi, 0, 0)),
                const(conv_w), const(g1), const(g2), const(wout), const(gf), const(wrh), const(wrl),
                const(rb), const(tri)]
    t = b * s
    return pl.pallas_call(
        _post_kernel,
        grid=(b, nt),
        in_specs=in_specs,
        out_specs=[tok(d), flat(d),
                   pl.BlockSpec((2 * TOP_K, tm), lambda bi, i: (0, bi * nt + i)),
                   flat(LANES),
                   pl.BlockSpec((N_EXPERTS, LANES), lambda bi, i: (0, 0))],
        out_shape=[jax.ShapeDtypeStruct((b, s, d), F32),
                   jax.ShapeDtypeStruct((t, d), F32),
                   jax.ShapeDtypeStruct((2 * TOP_K, t), jnp.int32),
                   jax.ShapeDtypeStruct((t, LANES), F32),
                   jax.ShapeDtypeStruct((N_EXPERTS, LANES), jnp.int32)],
        scratch_shapes=[pltpu.VMEM((N_EXPERTS, 1), F32)],
        compiler_params=pltpu.CompilerParams(dimension_semantics=("arbitrary", "arbitrary"),
                                             vmem_limit_bytes=VMEM_LIMIT),
        name="post",
    )(o_mla, gb, u, u, u, x, mods, conv_w, g1, g2, wout, gf, wrh, wrl, rb, tri)


def _positions_kernel(base_ref, ri_ref, pos_ref):
    e = ri_ref[0:TOP_K, :]
    pos = ri_ref[TOP_K:2 * TOP_K, :]
    for j in range(N_EXPERTS):
        pos = pos + jnp.where(e == j, base_ref[j], 0)
    pos_ref[...] = pos


def _positions(base, ri, *, tm):
    t = ri.shape[1]
    return pl.pallas_call(
        _positions_kernel,
        grid_spec=pltpu.PrefetchScalarGridSpec(
            num_scalar_prefetch=1,
            grid=(t // tm,),
            in_specs=[pl.BlockSpec((2 * TOP_K, tm), lambda i, base: (0, i))],
            out_specs=pl.BlockSpec((TOP_K, tm), lambda i, base: (0, i))),
        out_shape=jax.ShapeDtypeStruct((TOP_K, t), jnp.int32),
        compiler_params=pltpu.CompilerParams(dimension_semantics=("arbitrary",)),
        name="positions",
    )(base, ri)


def _dispatch_kernel(zt_ref, pos_ref, xn_ref, xs_ref, zero_ref, sem):
    tm = xn_ref.shape[0]
    tz = zero_ref.shape[0]

    @pl.when(pl.program_id(0) == 0)
    def _():
        zero_ref[...] = jnp.zeros_like(zero_ref)
        n_fill = zt_ref.shape[0]
        fill = lambda j: pltpu.make_async_copy(
            zero_ref, xs_ref.at[pl.ds(jnp.maximum(zt_ref[j], 0) * tz, tz)], sem)
        for j in range(n_fill):
            pl.when(zt_ref[j] >= 0)(fill(j).start)
        for j in range(n_fill):
            pl.when(zt_ref[j] >= 0)(fill(j).wait)

    def issue(t, carry):
        for k in range(TOP_K):
            pltpu.make_async_copy(xn_ref.at[pl.ds(t, 1)], xs_ref.at[pl.ds(pos_ref[k, t], 1)], sem).start()
        return carry

    lax.fori_loop(0, tm, issue, 0, unroll=DMA_UNROLL)
    for _ in range(TOP_K):
        pltpu.make_async_copy(xn_ref, xs_ref.at[pl.ds(0, tm)], sem).wait()


def _dispatch(zt, pos, xn, *, n_rows, tm, tile):
    t, w = xn.shape
    return pl.pallas_call(
        _dispatch_kernel,
        grid_spec=pltpu.PrefetchScalarGridSpec(
            num_scalar_prefetch=1,
            grid=(t // tm,),
            in_specs=[pl.BlockSpec((TOP_K, tm), lambda i, zt: (0, i), memory_space=pltpu.SMEM),
                      pl.BlockSpec((tm, w), lambda i, zt: (i, 0))],
            out_specs=pl.BlockSpec(memory_space=pl.ANY),
            scratch_shapes=[pltpu.VMEM((tile, w), xn.dtype), pltpu.SemaphoreType.DMA(())]),
        out_shape=jax.ShapeDtypeStruct((n_rows, w), xn.dtype),
        compiler_params=pltpu.CompilerParams(dimension_semantics=("arbitrary",),
                                             vmem_limit_bytes=VMEM_LIMIT),
        name="dispatch",
    )(zt, pos, xn)


def _expert_kernel(te_ref, tv_ref, xs_ref, wgu_ref, wd_ref, ys_ref):
    nv = tv_ref[pl.program_id(0)]
    ff = wd_ref.shape[1]

    @pl.when(nv > 0)
    def _():
        rows = lax.broadcasted_iota(jnp.int32, xs_ref.shape, 0)
        x = jnp.where(rows < nv, xs_ref[...], 0.0).astype(BF16)
        gu = _dot(x, wgu_ref[0])
        a = (_silu(gu[:, :ff]) * gu[:, ff:]).astype(BF16)
        ys_ref[...] = _dot(a, wd_ref[0])

    @pl.when(nv == 0)
    def _():
        ys_ref[...] = jnp.zeros_like(ys_ref)


def _experts(te, tv, xs, wgu, wd, *, tm):
    n_rows, w = xs.shape
    d = wd.shape[2]
    return pl.pallas_call(
        _expert_kernel,
        grid_spec=pltpu.PrefetchScalarGridSpec(
            num_scalar_prefetch=2,
            grid=(n_rows // tm,),
            in_specs=[pl.BlockSpec((tm, w), lambda i, te, tv: (i, 0)),
                      pl.BlockSpec((1,) + wgu.shape[1:], lambda i, te, tv: (te[i], 0, 0)),
                      pl.BlockSpec((1,) + wd.shape[1:], lambda i, te, tv: (te[i], 0, 0))],
            out_specs=pl.BlockSpec((tm, d), lambda i, te, tv: (i, 0))),
        out_shape=jax.ShapeDtypeStruct((n_rows, d), F32),
        compiler_params=pltpu.CompilerParams(dimension_semantics=("arbitrary",),
                                             vmem_limit_bytes=VMEM_LIMIT),
        name="experts",
    )(te, tv, xs, wgu, wd)


def _combine_kernel(pos_ref, pos_next_ref, rw_ref, xn_ref, h_ref, mods_ref, wsgu_ref, wsd_ref, gfin_ref,
                    ys_ref, o_ref, buf_ref, sems):
    tm = xn_ref.shape[0]
    g = pl.program_id(0) * pl.num_programs(1) + pl.program_id(1)
    n = pl.num_programs(0) * pl.num_programs(1)
    slot = g % 2

    def gather(p_ref, to):
        def issue(t, carry):
            for k in range(TOP_K):
                pltpu.make_async_copy(ys_ref.at[pl.ds(p_ref[k, t], 1)],
                                      buf_ref.at[to, k, pl.ds(t, 1)], sems.at[to]).start()
            return carry
        lax.fori_loop(0, tm, issue, 0, unroll=DMA_UNROLL)

    @pl.when(g == 0)
    def _():
        gather(pos_ref, 0)

    @pl.when(g + 1 < n)
    def _():
        gather(pos_next_ref, 1 - slot)

    sgu = _dot(xn_ref[...].astype(BF16), wsgu_ref[...])
    sf = sgu.shape[1] // 2
    y = _dot((_silu(sgu[:, :sf]) * sgu[:, sf:]).astype(BF16), wsd_ref[...])

    for k in range(TOP_K):
        pltpu.make_async_copy(ys_ref.at[pl.ds(0, tm)], buf_ref.at[slot, k], sems.at[slot]).wait()
    rw = rw_ref[...]
    for k in range(TOP_K):
        y = y + rw[:, k:k + 1] * buf_ref[slot, k]
    gate = mods_ref[0, 5:6, :]
    o_ref[0] = _rms(h_ref[0] + gate * y, gfin_ref[...])


def _combine(pos, rw, xn, h, mods, wsgu, wsd, gfin, ys, *, tm):
    b, s, d = h.shape
    nt = s // tm
    last = b * nt - 1
    const = lambda a: pl.BlockSpec(a.shape, lambda bi, i: (0,) * a.ndim, pipeline_mode=pl.Buffered(1))
    flat = lambda w: pl.BlockSpec((tm, w), lambda bi, i: (bi * nt + i, 0))
    return pl.pallas_call(
        _combine_kernel,
        grid=(b, nt),
        in_specs=[pl.BlockSpec((TOP_K, tm), lambda bi, i: (0, bi * nt + i), memory_space=pltpu.SMEM),
                  pl.BlockSpec((TOP_K, tm), lambda bi, i: (0, jnp.minimum(bi * nt + i + 1, last)),
                               memory_space=pltpu.SMEM),
                  flat(LANES), flat(xn.shape[1]),
                  pl.BlockSpec((1, tm, d), lambda bi, i: (bi, i, 0)),
                  pl.BlockSpec((1, N_MOD, d), lambda bi, i: (bi, 0, 0)),
                  const(wsgu), const(wsd), const(gfin),
                  pl.BlockSpec(memory_space=pl.ANY)],
        out_specs=pl.BlockSpec((1, tm, d), lambda bi, i: (bi, i, 0)),
        out_shape=jax.ShapeDtypeStruct((b, s, d), F32),
        scratch_shapes=[pltpu.VMEM((2, TOP_K, tm, d), F32), pltpu.SemaphoreType.DMA((2,))],
        compiler_params=pltpu.CompilerParams(dimension_semantics=("arbitrary", "arbitrary"),
                                             vmem_limit_bytes=VMEM_LIMIT),
        name="combine",
    )(pos, pos, rw, xn, h, mods, wsgu, wsd, gfin, ys)


def _rope_tables(s):
    nf = D_ROPE // 4
    pos = jnp.arange(s, dtype=jnp.int32)
    row = (pos // GRID_W).astype(F32)
    col = (pos % GRID_W).astype(F32)
    inv = ROPE_BASE ** (-jnp.arange(nf, dtype=F32) / nf)
    ar, ac = row[:, None] * inv, col[:, None] * inv
    c64 = jnp.concatenate([jnp.cos(ar), jnp.cos(ar), jnp.cos(ac), jnp.cos(ac)], axis=-1)
    s64 = jnp.concatenate([-jnp.sin(ar), jnp.sin(ar), -jnp.sin(ac), jnp.sin(ac)], axis=-1)
    return c64, s64


def _swap_pairs(w):
    nf = D_ROPE // 4
    return jnp.concatenate([w[..., nf:2 * nf], w[..., :nf], w[..., 3 * nf:], w[..., 2 * nf:3 * nf]], axis=-1)


def kernel(x, c, ctx, c_ctx, w_mod, b_mod, attn_norm, w_in, q_a_norm, w_q_b, kv_a_norm, w_kv_b, conv_w,
           o_norm_mla, o_norm_conv, w_out, ffn_norm, w_router, router_bias, w_exp_gate, w_exp_up,
           w_exp_down, w_sh_gate, w_sh_up, w_sh_down, final_norm):
    b, s, d = x.shape
    lc = ctx.shape[1]
    assert w_mod.shape[0] == 1, "single trunk layer"
    assert s % GRID_W == 0
    assert lc <= TM_IN, "context keys are processed as a single attention chunk"
    tm_in, tm_post = min(TM_IN, s), min(TM_POST, s)

    wi = w_in[0]
    kr_cols = wi[:, C_KR:C_KR + D_ROPE]
    win = jnp.concatenate([wi[:, :C_KR], kr_cols, _swap_pairs(kr_cols), wi[:, C_KR + D_ROPE:]],
                          axis=-1).astype(BF16)
    wq3 = w_q_b[0].reshape(Q_RANK, N_HEADS, D_NOPE + D_ROPE)
    wq = jnp.concatenate([wq3, _swap_pairs(wq3[..., D_NOPE:])], axis=-1)
    wqt = wq.reshape(Q_RANK, N_HEADS * D_QK_PAD).T.astype(BF16)
    wkv3 = w_kv_b[0].reshape(KV_RANK, N_HEADS, D_NOPE + D_V)
    wk = wkv3[..., :D_NOPE].reshape(KV_RANK, MLA_W).astype(BF16)
    wvt = wkv3[..., D_NOPE:].reshape(KV_RANK, MLA_W).T.astype(BF16)
    wr = jnp.pad(w_router[0], ((0, 0), (0, LANES - N_EXPERTS)))
    wrh, wrl = _split_bf16(wr)
    wgu = jnp.concatenate([w_exp_gate[0], w_exp_up[0]], axis=-1).astype(BF16)
    wd = w_exp_down[0].astype(BF16)
    wsgu = jnp.concatenate([w_sh_gate[0], w_sh_up[0]], axis=-1).astype(BF16)
    wsd = w_sh_down[0].astype(BF16)
    row = lambda v: v.reshape(1, -1)

    n_rows = -(-(b + 1) // 8) * 8
    cc = jnp.zeros((n_rows, d), F32).at[:b].set(c).at[b].set(c_ctx)
    mods = _mods(cc, w_mod[0], row(b_mod[0])).reshape(n_rows, N_MOD, d)

    c64, s64 = _rope_tables(s)
    pad = lambda t: jnp.pad(t, ((0, 0), (0, LANES - D_ROPE)))
    ones_c, zeros_c = jnp.ones((lc, D_ROPE), F32), jnp.zeros((lc, D_ROPE), F32)
    common = (row(attn_norm[0]), win, wqt, wk, wvt, row(q_a_norm[0]), row(kv_a_norm[0]))
    kc, vct = _inproj(ctx, mods, lambda bi: b, *common, pad(ones_c), pad(zeros_c), ones_c.T, zeros_c.T,
                      latent=False, tm=min(TM_IN, lc))
    qt, kx, vxt, gb, u = _inproj(x, mods, lambda bi: bi, *common, pad(c64), pad(s64), c64.T, s64.T,
                                 latent=True, tm=tm_in)

    o_mla = _attention(qt, kc, vct, kx, vxt)

    h, xn, ri, rw, cnt = _post(o_mla, gb, u, x, mods, conv_w[0], row(o_norm_mla[0]), row(o_norm_conv[0]),
                               w_out[0].astype(BF16), row(ffn_norm[0]), wrh, wrl,
                               router_bias[0].reshape(-1, 1), tm=tm_post)

    tmx = TM_EXPERT
    n_tiles = pl.cdiv(b * s * TOP_K, tmx) + N_EXPERTS
    counts = cnt[:, 0]
    tiles_per = (counts + tmx - 1) // tmx
    tile_end = jnp.cumsum(tiles_per)
    tile_start = tile_end - tiles_per
    base = (tile_start * tmx).astype(jnp.int32)
    tile_ids = jnp.arange(n_tiles, dtype=jnp.int32)
    te = jnp.minimum(jnp.sum(tile_end[None, :] <= tile_ids[:, None], axis=1), N_EXPERTS - 1)
    own = te[:, None] == jnp.arange(N_EXPERTS)[None, :]
    left = jnp.sum(jnp.where(own, counts[None, :] - (tile_ids[:, None] - tile_start[None, :]) * tmx, 0),
                   axis=1)
    tv = jnp.where(tile_ids < tile_end[-1], jnp.clip(left, 0, tmx), 0).astype(jnp.int32)
    tail = tile_end[-1] + jnp.arange(N_EXPERTS)
    zt = jnp.concatenate([jnp.where(tiles_per > 0, tile_end - 1, -1),
                          jnp.where(tail < n_tiles, tail, -1)]).astype(jnp.int32)

    pos = _positions(base, ri, tm=min(TM_POSITIONS, b * s))
    xs = _dispatch(zt, pos, xn, n_rows=n_tiles * tmx, tm=min(TM_DISPATCH, s), tile=tmx)
    ys = _experts(te.astype(jnp.int32), tv, xs, wgu, wd, tm=tmx)
    return _combine(pos, rw, xn, h, mods, wsgu, wsd, row(final_norm), ys, tm=min(TM_COMBINE, s))
```

```python
import functools
import math

import jax
import jax.numpy as jnp
from jax import lax
from jax.experimental import pallas as pl
from jax.experimental.pallas import tpu as pltpu

F32 = jnp.float32
BF16 = jnp.bfloat16

N_HEADS = 8
D_NOPE = 128
D_ROPE = 64
D_V = 128
D_QK_PAD = 256
Q_RANK = 512
KV_RANK = 256
MLA_W = N_HEADS * D_V
GRID_W = 64
ROPE_BASE = 10000.0
ATTN_SCALE = (D_NOPE + D_ROPE) ** -0.5
LOG2E = math.log2(math.e)
N_EXPERTS = 32
N_GROUPS = 4
GROUP_SIZE = N_EXPERTS // N_GROUPS
TOPK_GROUPS = 2
TOP_K = 4
ROUTED_SCALE = 2.5
N_MOD = 6
EPS = 1e-6

C_Q = 0
C_KV = Q_RANK
C_KR = C_KV + KV_RANK
C_GB = C_KR + 128

LANES = 128
BF16_SUBLANES = 16
VMEM_LIMIT = 56 * 1024 * 1024

TM_IN = 512
ATTN_Q_TILES = 4
TM_POST = 512
TM_POSITIONS = 4096
TM_DISPATCH = 512
TM_EXPERT = 512
TM_COMBINE = 256
TN_MODS = 1024
DMA_UNROLL = 8
COMBINE_SLICES = 8


def _rms(x, g):
    return x * lax.rsqrt(jnp.mean(x * x, axis=-1, keepdims=True) + EPS) * g


def _silu(x):
    return x / (1.0 + jnp.exp(-x))


def _split_bf16(x):
    hi = x.astype(BF16)
    lo = (x - hi.astype(F32)).astype(BF16)
    return hi, lo


def _dot(a, b):
    return jnp.dot(a, b, preferred_element_type=F32)


def _mods_kernel(a_ref, w_ref, b_ref, o_ref):
    a = _silu(a_ref[...])
    ah, al = _split_bf16(a)
    wh, wl = _split_bf16(w_ref[...])
    o_ref[...] = _dot(ah, wh) + _dot(al, wh) + _dot(ah, wl) + b_ref[...]


def _mods(cc, w_mod, b_mod):
    rows, d = cc.shape
    n = w_mod.shape[1]
    return pl.pallas_call(
        _mods_kernel,
        grid=(n // TN_MODS,),
        in_specs=[pl.BlockSpec((rows, d), lambda j: (0, 0)),
                  pl.BlockSpec((d, TN_MODS), lambda j: (0, j)),
                  pl.BlockSpec((1, TN_MODS), lambda j: (0, j))],
        out_specs=pl.BlockSpec((rows, TN_MODS), lambda j: (0, j)),
        out_shape=jax.ShapeDtypeStruct((rows, n), F32),
        compiler_params=pltpu.CompilerParams(dimension_semantics=("arbitrary",),
                                             vmem_limit_bytes=VMEM_LIMIT),
        name="mods",
    )(cc, w_mod, b_mod)


def _inproj_kernel(x_ref, mods_ref, g_ref, win_ref, wqt_ref, wk_ref, wvt_ref, qg_ref, kvg_ref,
                   ca_ref, sa_ref, ct_ref, st_ref, *out_refs, latent):
    x = x_ref[0]
    shift = mods_ref[0, 0:1, :]
    scale = mods_ref[0, 1:2, :]
    nx = (_rms(x, g_ref[...]) * (1.0 + scale) + shift).astype(BF16)

    if latent:
        qt_ref, k_ref, vt_ref, gb_ref, u_ref = out_refs
        q_a = _dot(nx, win_ref[:, C_Q:C_KV])
        qnt = _rms(q_a, qg_ref[...]).T.astype(BF16)
        qt = _dot(wqt_ref[...], qnt) * (ATTN_SCALE * LOG2E)
        ct = ct_ref[...]
        st = st_ref[...]
        for h in range(N_HEADS):
            r0 = h * D_QK_PAD
            r1 = r0 + D_NOPE
            r2 = r1 + D_ROPE
            qt_ref[0, 0, r0:r1, :] = qt[r0:r1].astype(BF16)
            qt_ref[0, 0, r1:r2, :] = (qt[r1:r2] * ct + qt[r2:r0 + D_QK_PAD] * st).astype(BF16)
            qt_ref[0, 0, r2:r0 + D_QK_PAD, :] = jnp.zeros((D_ROPE, qt.shape[1]), BF16)
        gb_ref[0] = _dot(nx, win_ref[:, C_GB:C_GB + MLA_W]).astype(BF16)
        g_c = _dot(nx, win_ref[:, C_GB + MLA_W:C_GB + 2 * MLA_W])
        hh = _dot(nx, win_ref[:, C_GB + 2 * MLA_W:C_GB + 3 * MLA_W])
        u_ref[0] = (g_c * hh).astype(BF16)
    else:
        k_ref, vt_ref = out_refs

    kv_a = _dot(nx, win_ref[:, C_KV:C_KR])
    kvn = _rms(kv_a, kvg_ref[...])
    kn = _dot(kvn.astype(BF16), wk_ref[...])
    vt_ref[0, 0] = _dot(wvt_ref[...], kvn.T.astype(BF16)).astype(BF16)
    blk = _dot(nx, win_ref[:, C_KR:C_GB])
    kr = (blk * ca_ref[...] + pltpu.roll(blk, D_ROPE, axis=1) * sa_ref[...]).astype(BF16)
    for h in range(N_HEADS):
        c0 = h * D_QK_PAD
        k_ref[0, :, c0:c0 + D_NOPE] = kn[:, h * D_NOPE:(h + 1) * D_NOPE].astype(BF16)
        k_ref[0, :, c0 + D_NOPE:c0 + D_QK_PAD] = kr


def _inproj(x, mods, mod_row, g, win, wqt, wk, wvt, qg, kvg, ca, sa, ct, st, *, latent, tm):
    b, s, d = x.shape
    nt = s // tm
    const = lambda a: pl.BlockSpec(a.shape, lambda bi, i: (0,) * a.ndim, pipeline_mode=pl.Buffered(1))
    tok = lambda w: pl.BlockSpec((1, tm, w), lambda bi, i: (bi, i, 0))
    tok_t = lambda w: pl.BlockSpec((1, 1, w, tm), lambda bi, i: (bi, i, 0, 0))
    in_specs = [tok(d),
                pl.BlockSpec((1, N_MOD, d), lambda bi, i: (mod_row(bi), 0, 0)),
                const(g), const(win), const(wqt), const(wk), const(wvt), const(qg), const(kvg),
                pl.BlockSpec((tm, LANES), lambda bi, i: (i, 0)),
                pl.BlockSpec((tm, LANES), lambda bi, i: (i, 0)),
                pl.BlockSpec((D_ROPE, tm), lambda bi, i: (0, i)),
                pl.BlockSpec((D_ROPE, tm), lambda bi, i: (0, i))]
    kq = N_HEADS * D_QK_PAD
    row_major = lambda w: (tok(w), jax.ShapeDtypeStruct((b, s, w), BF16))
    col_major = lambda w: (tok_t(w), jax.ShapeDtypeStruct((b, nt, w, tm), BF16))
    outs = [row_major(kq), col_major(MLA_W)]
    if latent:
        outs = [col_major(kq)] + outs + [row_major(MLA_W), row_major(MLA_W)]
    return pl.pallas_call(
        functools.partial(_inproj_kernel, latent=latent),
        grid=(b, nt),
        in_specs=in_specs,
        out_specs=[o[0] for o in outs],
        out_shape=[o[1] for o in outs],
        compiler_params=pltpu.CompilerParams(dimension_semantics=("parallel", "arbitrary"),
                                             vmem_limit_bytes=VMEM_LIMIT),
        name="inproj_latent" if latent else "inproj_ctx",
    )(x, mods, g, win, wqt, wk, wvt, qg, kvg, ca, sa, ct, st)


def _attn_kernel(qt_ref, kc_ref, vct_ref, kx_ref, vxt_ref, o_ref, s_ref):
    qt = jnp.concatenate([qt_ref[0, j] for j in range(qt_ref.shape[1])], axis=1)
    tq = qt.shape[1]
    nb = vxt_ref.shape[1]
    tk = kx_ref.shape[1] // nb
    lc = kc_ref.shape[1]

    def scores(k):
        s = _dot(k, qt)
        return s, jnp.max(s, axis=0, keepdims=True)

    def keys(j):
        return kx_ref[0, pl.ds(pl.multiple_of(j * tk, tk), tk), :]

    def absorb(s, mx, vt, carry):
        m, l, acc = carry
        m_new = jnp.maximum(m, mx)
        alpha = jnp.exp2(m - m_new)
        p = jnp.exp2(s - m_new)
        l = alpha * l + jnp.sum(p, axis=0, keepdims=True)
        acc = alpha * acc + _dot(vt, p.astype(BF16))
        return m_new, l, acc

    def pair(cur_rows, cur_vt, j, carry, mx):
        s_n, mx_n = scores(keys(j))
        s_ref[1] = s_n
        carry = absorb(s_ref[0, :cur_rows, :], mx, cur_vt, carry)
        s_a, mx_a = scores(keys(j + 1))
        s_ref[0] = s_a
        carry = absorb(s_ref[1], mx_n, vxt_ref[0, j], carry)
        return carry, mx_a

    init = (jnp.full((1, tq), -jnp.inf, F32), jnp.zeros((1, tq), F32), jnp.zeros((D_V, tq), F32))
    s, mx = scores(kc_ref[0])
    s_ref[0, :lc, :] = s
    state = pair(lc, vct_ref[0, 0], 0, init, mx)

    def body(i, state):
        return pair(tk, vxt_ref[0, 2 * i - 1], 2 * i, *state)

    carry, mx = lax.fori_loop(1, nb // 2, body, state)
    _, l, acc = absorb(s_ref[0], mx, vxt_ref[0, nb - 1], carry)
    o_ref[0] = (acc / l).T.astype(o_ref.dtype)


def _attention(qt, kc, vct, kx, vxt):
    b, nt, _, tk = qt.shape
    s = kx.shape[1]
    lc = kc.shape[1]
    nq = min(ATTN_Q_TILES, nt)
    tq = nq * tk
    assert nt % 2 == 0 and nt % nq == 0 and lc <= tk, \
        "latent chunks are absorbed in pairs after one context chunk"
    return pl.pallas_call(
        _attn_kernel,
        grid=(b, N_HEADS, nt // nq),
        in_specs=[pl.BlockSpec((1, nq, D_QK_PAD, tk), lambda bi, h, i: (bi, i, h, 0)),
                  pl.BlockSpec((1, lc, D_QK_PAD), lambda bi, h, i: (bi, 0, h)),
                  pl.BlockSpec((1, 1, D_V, lc), lambda bi, h, i: (bi, 0, h, 0)),
                  pl.BlockSpec((1, s, D_QK_PAD), lambda bi, h, i: (bi, 0, h)),
                  pl.BlockSpec((1, nt, D_V, tk), lambda bi, h, i: (bi, 0, h, 0))],
        out_specs=pl.BlockSpec((1, tq, D_V), lambda bi, h, i: (bi, i, h)),
        out_shape=jax.ShapeDtypeStruct((b, s, MLA_W), BF16),
        scratch_shapes=[pltpu.VMEM((2, tk, tq), F32)],
        compiler_params=pltpu.CompilerParams(
            dimension_semantics=("parallel", "parallel", "arbitrary"), vmem_limit_bytes=VMEM_LIMIT),
        name="attn",
    )(qt, kc, vct, kx, vxt)


def _route(logits_t, bias):
    e, tm = logits_t.shape
    neg = -jnp.inf
    s = 1.0 / (1.0 + jnp.exp(-logits_t))
    sel = s + bias
    io_g = lax.broadcasted_iota(jnp.int32, (GROUP_SIZE, tm), 0)
    gs = []
    for g in range(N_GROUPS):
        blk = sel[g * GROUP_SIZE:(g + 1) * GROUP_SIZE]
        m1 = jnp.max(blk, axis=0, keepdims=True)
        i1 = jnp.min(jnp.where(blk == m1, io_g, GROUP_SIZE), axis=0, keepdims=True)
        m2 = jnp.max(jnp.where(io_g == i1, neg, blk), axis=0, keepdims=True)
        gs.append(m1 + m2)
    masks = []
    for g in range(N_GROUPS):
        rank = jnp.zeros((1, tm), jnp.int32)
        for o in range(N_GROUPS):
            if o == g:
                continue
            ahead = (gs[o] > gs[g]) | (gs[o] == gs[g]) if o < g else (gs[o] > gs[g])
            rank = rank + ahead.astype(jnp.int32)
        masks.append(jnp.broadcast_to(rank < TOPK_GROUPS, (GROUP_SIZE, tm)))
    emask = jnp.concatenate(masks, axis=0)
    cur = jnp.where(emask, sel, neg)
    io_e = lax.broadcasted_iota(jnp.int32, (e, tm), 0)
    chosen = jnp.zeros((e, tm), jnp.bool_)
    slots = []
    for _ in range(TOP_K):
        m = jnp.max(cur, axis=0, keepdims=True)
        i = jnp.min(jnp.where(cur == m, io_e, e), axis=0, keepdims=True)
        hit = io_e == i
        chosen = chosen | hit
        cur = jnp.where(hit, neg, cur)
        slots.append((hit, i))
    wsum = jnp.sum(jnp.where(chosen, s, 0.0), axis=0, keepdims=True)
    return chosen, jnp.where(chosen, s / wsum * ROUTED_SCALE, 0.0), slots


def _post_kernel(om_ref, gb_ref, u_ref, up_ref, un_ref, x_ref, mods_ref, cw_ref, g1_ref, g2_ref,
                 wout_ref, gf_ref, wr_ref, rb_ref, tri_ref,
                 h_ref, xn_ref, ri_ref, rw_ref, cnt_ref, run_ref):
    first = (pl.program_id(0) == 0) & (pl.program_id(1) == 0)

    @pl.when(first)
    def _():
        run_ref[...] = jnp.zeros_like(run_ref)

    i = pl.program_id(1)
    nt = pl.num_programs(1)
    tm = u_ref.shape[1]
    u = u_ref[0].astype(F32)
    prev = jnp.where(i > 0, up_ref[0].astype(F32)[BF16_SUBLANES - 1:BF16_SUBLANES], 0.0)
    nxt = jnp.where(i < nt - 1, un_ref[0].astype(F32)[0:1], 0.0)
    row = lax.broadcasted_iota(jnp.int32, u.shape, 0)
    u_m1 = jnp.where(row == 0, prev, pltpu.roll(u, 1, axis=0))
    u_p1 = jnp.where(row == tm - 1, nxt, pltpu.roll(u, tm - 1, axis=0))
    cw = cw_ref[...]
    y = gb_ref[0].astype(F32) * (cw[0:1] * u_m1 + cw[1:2] * u + cw[2:3] * u_p1)
    o = jnp.concatenate([_rms(om_ref[0].astype(F32), g1_ref[...]).astype(BF16),
                         _rms(y, g2_ref[...]).astype(BF16)], axis=-1)
    ga = mods_ref[0, 2:3, :]
    sf = mods_ref[0, 3:4, :]
    scf = mods_ref[0, 4:5, :]
    h = x_ref[0] + ga * _dot(o, wout_ref[...])
    h_ref[0] = h
    xn = _rms(h, gf_ref[...]) * (1.0 + scf) + sf
    xh, xl = _split_bf16(xn)
    xn_ref[...] = xn
    hh_hl = _dot(xh, wr_ref[...])
    logits = hh_hl[:, :LANES] + hh_hl[:, LANES:] + _dot(xl, wr_ref[:, :LANES])
    chosen, comb_t, slots = _route(logits.T[:N_EXPERTS], rb_ref[...])
    sel = jnp.where(chosen, 1.0, 0.0)
    before = _dot(sel.astype(BF16), tri_ref[...]) + run_ref[...]
    run_ref[...] += jnp.sum(sel, axis=1, keepdims=True)
    cnt_ref[...] = jnp.broadcast_to(run_ref[...], cnt_ref.shape).astype(jnp.int32)
    pick = lambda hit, v: jnp.sum(jnp.where(hit, v, 0.0), axis=0, keepdims=True)
    ri_ref[...] = jnp.concatenate(
        [idx for _, idx in slots] + [pick(hit, before).astype(jnp.int32) for hit, _ in slots], axis=0)
    rw_ref[...] = jnp.concatenate(
        [pick(hit, comb_t) for hit, _ in slots] + [jnp.zeros((LANES - TOP_K, tm), F32)], axis=0).T


def _post(o_mla, gb, u, x, mods, conv_w, g1, g2, wout, gf, wr, rb, *, tm):
    b, s, d = x.shape
    nt = s // tm
    hb = tm // BF16_SUBLANES
    last = s // BF16_SUBLANES - 1
    tri = (lax.broadcasted_iota(jnp.int32, (tm, tm), 0)
           < lax.broadcasted_iota(jnp.int32, (tm, tm), 1)).astype(BF16)
    const = lambda a: pl.BlockSpec(a.shape, lambda bi, i: (0,) * a.ndim, pipeline_mode=pl.Buffered(1))
    tok = lambda w: pl.BlockSpec((1, tm, w), lambda bi, i: (bi, i, 0))
    flat = lambda w: pl.BlockSpec((tm, w), lambda bi, i: (bi * nt + i, 0))
    in_specs = [tok(MLA_W), tok(MLA_W), tok(MLA_W),
                pl.BlockSpec((1, BF16_SUBLANES, MLA_W), lambda bi, i: (bi, jnp.maximum(i * hb - 1, 0), 0)),
                pl.BlockSpec((1, BF16_SUBLANES, MLA_W), lambda bi, i: (bi, jnp.minimum((i + 1) * hb, last), 0)),
                tok(d),
                pl.BlockSpec((1, N_MOD, d), lambda bi, i: (bi, 0, 0)),
                const(conv_w), const(g1), const(g2), const(wout), const(gf), const(wr),
                const(rb), const(tri)]
    t = b * s
    return pl.pallas_call(
        _post_kernel,
        grid=(b, nt),
        in_specs=in_specs,
        out_specs=[tok(d), flat(d),
                   pl.BlockSpec((2 * TOP_K, tm), lambda bi, i: (0, bi * nt + i)),
                   flat(LANES),
                   pl.BlockSpec((N_EXPERTS, LANES), lambda bi, i: (0, 0))],
        out_shape=[jax.ShapeDtypeStruct((b, s, d), F32),
                   jax.ShapeDtypeStruct((t, d), F32),
                   jax.ShapeDtypeStruct((2 * TOP_K, t), jnp.int32),
                   jax.ShapeDtypeStruct((t, LANES), F32),
                   jax.ShapeDtypeStruct((N_EXPERTS, LANES), jnp.int32)],
        scratch_shapes=[pltpu.VMEM((N_EXPERTS, 1), F32)],
        compiler_params=pltpu.CompilerParams(dimension_semantics=("arbitrary", "arbitrary"),
                                             vmem_limit_bytes=VMEM_LIMIT),
        name="post",
    )(o_mla, gb, u, u, u, x, mods, conv_w, g1, g2, wout, gf, wr, rb, tri)


def _positions_kernel(base_ref, ri_ref, pos_ref):
    e = ri_ref[0:TOP_K, :]
    pos = ri_ref[TOP_K:2 * TOP_K, :]
    for j in range(N_EXPERTS):
        pos = pos + jnp.where(e == j, base_ref[j], 0)
    pos_ref[...] = pos


def _positions(base, ri, *, tm):
    t = ri.shape[1]
    return pl.pallas_call(
        _positions_kernel,
        grid_spec=pltpu.PrefetchScalarGridSpec(
            num_scalar_prefetch=1,
            grid=(t // tm,),
            in_specs=[pl.BlockSpec((2 * TOP_K, tm), lambda i, base: (0, i))],
            out_specs=pl.BlockSpec((TOP_K, tm), lambda i, base: (0, i))),
        out_shape=jax.ShapeDtypeStruct((TOP_K, t), jnp.int32),
        compiler_params=pltpu.CompilerParams(dimension_semantics=("arbitrary",)),
        name="positions",
    )(base, ri)


def _dispatch_kernel(zt_ref, pos_ref, xn_ref, xs_ref, zero_ref, sem):
    tm = xn_ref.shape[0]
    tz = zero_ref.shape[0]

    @pl.when(pl.program_id(0) == 0)
    def _():
        zero_ref[...] = jnp.zeros_like(zero_ref)
        n_fill = zt_ref.shape[0]
        fill = lambda j: pltpu.make_async_copy(
            zero_ref, xs_ref.at[pl.ds(jnp.maximum(zt_ref[j], 0) * tz, tz)], sem)
        for j in range(n_fill):
            pl.when(zt_ref[j] >= 0)(fill(j).start)
        for j in range(n_fill):
            pl.when(zt_ref[j] >= 0)(fill(j).wait)

    def issue(t, carry):
        for k in range(TOP_K):
            pltpu.make_async_copy(xn_ref.at[pl.ds(t, 1)], xs_ref.at[pl.ds(pos_ref[k, t], 1)],
                                  sem).start(priority=k % 2)
        return carry

    lax.fori_loop(0, tm, issue, 0, unroll=DMA_UNROLL)
    for _ in range(TOP_K):
        pltpu.make_async_copy(xn_ref, xs_ref.at[pl.ds(0, tm)], sem).wait()


def _dispatch(zt, pos, xn, *, n_rows, tm, tile):
    t, w = xn.shape
    return pl.pallas_call(
        _dispatch_kernel,
        grid_spec=pltpu.PrefetchScalarGridSpec(
            num_scalar_prefetch=1,
            grid=(t // tm,),
            in_specs=[pl.BlockSpec((TOP_K, tm), lambda i, zt: (0, i), memory_space=pltpu.SMEM),
                      pl.BlockSpec((tm, w), lambda i, zt: (i, 0))],
            out_specs=pl.BlockSpec(memory_space=pl.ANY),
            scratch_shapes=[pltpu.VMEM((tile, w), xn.dtype), pltpu.SemaphoreType.DMA(())]),
        out_shape=jax.ShapeDtypeStruct((n_rows, w), xn.dtype),
        compiler_params=pltpu.CompilerParams(dimension_semantics=("arbitrary",),
                                             vmem_limit_bytes=VMEM_LIMIT),
        name="dispatch",
    )(zt, pos, xn)


def _expert_kernel(te_ref, tv_ref, xs_ref, wgu_ref, wd_ref, ys_ref):
    nv = tv_ref[pl.program_id(0)]
    ff = wd_ref.shape[1]

    @pl.when(nv > 0)
    def _():
        rows = lax.broadcasted_iota(jnp.int32, xs_ref.shape, 0)
        x = jnp.where(rows < nv, xs_ref[...], 0.0).astype(BF16)
        gu = _dot(x, wgu_ref[0])
        a = (_silu(gu[:, :ff]) * gu[:, ff:]).astype(BF16)
        ys_ref[...] = _dot(a, wd_ref[0])

    @pl.when(nv == 0)
    def _():
        ys_ref[...] = jnp.zeros_like(ys_ref)


def _experts(te, tv, xs, wgu, wd, *, tm):
    n_rows, w = xs.shape
    d = wd.shape[2]
    return pl.pallas_call(
        _expert_kernel,
        grid_spec=pltpu.PrefetchScalarGridSpec(
            num_scalar_prefetch=2,
            grid=(n_rows // tm,),
            in_specs=[pl.BlockSpec((tm, w), lambda i, te, tv: (i, 0)),
                      pl.BlockSpec((1,) + wgu.shape[1:], lambda i, te, tv: (te[i], 0, 0)),
                      pl.BlockSpec((1,) + wd.shape[1:], lambda i, te, tv: (te[i], 0, 0))],
            out_specs=pl.BlockSpec((tm, d), lambda i, te, tv: (i, 0))),
        out_shape=jax.ShapeDtypeStruct((n_rows, d), F32),
        compiler_params=pltpu.CompilerParams(dimension_semantics=("arbitrary",),
                                             vmem_limit_bytes=VMEM_LIMIT),
        name="experts",
    )(te, tv, xs, wgu, wd)


def _combine_kernel(pos_ref, pos_next_ref, rw_ref, xn_ref, h_ref, mods_ref, wsgu_ref, wsd_ref, gfin_ref,
                    ys_ref, o_ref, buf_ref, sems):
    tm, d = xn_ref.shape
    g = pl.program_id(0) * pl.num_programs(1) + pl.program_id(1)
    n = pl.num_programs(0) * pl.num_programs(1)
    slot = g % 2
    other = 1 - slot

    def start_rows(p_ref, t, to):
        for k in range(TOP_K):
            pltpu.make_async_copy(ys_ref.at[pl.ds(p_ref[k, t], 1)],
                                  buf_ref.at[to, k, pl.ds(t, 1)], sems.at[to]).start(priority=k % 2)

    def wait_rows(to):
        for k in range(TOP_K):
            pltpu.make_async_copy(ys_ref.at[pl.ds(0, tm)], buf_ref.at[to, k], sems.at[to]).wait()

    @pl.when(g == 0)
    def _():
        def issue(t, carry):
            start_rows(pos_ref, t, 0)
            return carry
        lax.fori_loop(0, tm, issue, 0, unroll=DMA_UNROLL)

    xb = xn_ref[...].astype(BF16)
    kc, tc = d // COMBINE_SLICES, tm // COMBINE_SLICES
    sgu = None
    for j in range(COMBINE_SLICES):
        part = _dot(xb[:, j * kc:(j + 1) * kc], wsgu_ref[j * kc:(j + 1) * kc, :])
        sgu = part if sgu is None else sgu + part
        for t in range(j * tc, (j + 1) * tc):
            start_rows(pos_next_ref, t, other)
    sf = sgu.shape[1] // 2
    y = _dot((_silu(sgu[:, :sf]) * sgu[:, sf:]).astype(BF16), wsd_ref[...])

    wait_rows(slot)
    rw = rw_ref[...]
    for k in range(TOP_K):
        y = y + rw[:, k:k + 1] * buf_ref[slot, k]
    gate = mods_ref[0, 5:6, :]
    o_ref[0] = _rms(h_ref[0] + gate * y, gfin_ref[...])

    @pl.when(g == n - 1)
    def _():
        wait_rows(other)


def _combine(pos, rw, xn, h, mods, wsgu, wsd, gfin, ys, *, tm):
    b, s, d = h.shape
    nt = s // tm
    last = b * nt - 1
    const = lambda a: pl.BlockSpec(a.shape, lambda bi, i: (0,) * a.ndim, pipeline_mode=pl.Buffered(1))
    flat = lambda w: pl.BlockSpec((tm, w), lambda bi, i: (bi * nt + i, 0))
    return pl.pallas_call(
        _combine_kernel,
        grid=(b, nt),
        in_specs=[pl.BlockSpec((TOP_K, tm), lambda bi, i: (0, bi * nt + i), memory_space=pltpu.SMEM),
                  pl.BlockSpec((TOP_K, tm), lambda bi, i: (0, jnp.minimum(bi * nt + i + 1, last)),
                               memory_space=pltpu.SMEM),
                  flat(LANES), flat(xn.shape[1]),
                  pl.BlockSpec((1, tm, d), lambda bi, i: (bi, i, 0)),
                  pl.BlockSpec((1, N_MOD, d), lambda bi, i: (bi, 0, 0)),
                  const(wsgu), const(wsd), const(gfin),
                  pl.BlockSpec(memory_space=pl.ANY)],
        out_specs=pl.BlockSpec((1, tm, d), lambda bi, i: (bi, i, 0)),
        out_shape=jax.ShapeDtypeStruct((b, s, d), F32),
        scratch_shapes=[pltpu.VMEM((2, TOP_K, tm, d), F32), pltpu.SemaphoreType.DMA((2,))],
        compiler_params=pltpu.CompilerParams(dimension_semantics=("arbitrary", "arbitrary"),
                                             vmem_limit_bytes=VMEM_LIMIT),
        name="combine",
    )(pos, pos, rw, xn, h, mods, wsgu, wsd, gfin, ys)


def _rope_tables(s):
    nf = D_ROPE // 4
    pos = jnp.arange(s, dtype=jnp.int32)
    row = (pos // GRID_W).astype(F32)
    col = (pos % GRID_W).astype(F32)
    inv = ROPE_BASE ** (-jnp.arange(nf, dtype=F32) / nf)
    ar, ac = row[:, None] * inv, col[:, None] * inv
    c64 = jnp.concatenate([jnp.cos(ar), jnp.cos(ar), jnp.cos(ac), jnp.cos(ac)], axis=-1)
    s64 = jnp.concatenate([-jnp.sin(ar), jnp.sin(ar), -jnp.sin(ac), jnp.sin(ac)], axis=-1)
    return c64, s64


def _swap_pairs(w):
    nf = D_ROPE // 4
    return jnp.concatenate([w[..., nf:2 * nf], w[..., :nf], w[..., 3 * nf:], w[..., 2 * nf:3 * nf]], axis=-1)


def kernel(x, c, ctx, c_ctx, w_mod, b_mod, attn_norm, w_in, q_a_norm, w_q_b, kv_a_norm, w_kv_b, conv_w,
           o_norm_mla, o_norm_conv, w_out, ffn_norm, w_router, router_bias, w_exp_gate, w_exp_up,
           w_exp_down, w_sh_gate, w_sh_up, w_sh_down, final_norm):
    b, s, d = x.shape
    lc = ctx.shape[1]
    assert w_mod.shape[0] == 1, "single trunk layer"
    assert s % GRID_W == 0
    assert lc <= TM_IN, "context keys are processed as a single attention chunk"
    tm_in, tm_post = min(TM_IN, s), min(TM_POST, s)

    wi = w_in[0]
    kr_cols = wi[:, C_KR:C_KR + D_ROPE]
    win = jnp.concatenate([wi[:, :C_KR], kr_cols, _swap_pairs(kr_cols), wi[:, C_KR + D_ROPE:]],
                          axis=-1).astype(BF16)
    wq3 = w_q_b[0].reshape(Q_RANK, N_HEADS, D_NOPE + D_ROPE)
    wq = jnp.concatenate([wq3, _swap_pairs(wq3[..., D_NOPE:])], axis=-1)
    wqt = wq.reshape(Q_RANK, N_HEADS * D_QK_PAD).T.astype(BF16)
    wkv3 = w_kv_b[0].reshape(KV_RANK, N_HEADS, D_NOPE + D_V)
    wk = wkv3[..., :D_NOPE].reshape(KV_RANK, MLA_W).astype(BF16)
    wvt = wkv3[..., D_NOPE:].reshape(KV_RANK, MLA_W).T.astype(BF16)
    wr = jnp.pad(w_router[0], ((0, 0), (0, LANES - N_EXPERTS)))
    wr = jnp.concatenate(_split_bf16(wr), axis=-1)
    wgu = jnp.concatenate([w_exp_gate[0], w_exp_up[0]], axis=-1).astype(BF16)
    wd = w_exp_down[0].astype(BF16)
    wsgu = jnp.concatenate([w_sh_gate[0], w_sh_up[0]], axis=-1).astype(BF16)
    wsd = w_sh_down[0].astype(BF16)
    row = lambda v: v.reshape(1, -1)

    n_rows = -(-(b + 1) // 8) * 8
    cc = jnp.zeros((n_rows, d), F32).at[:b].set(c).at[b].set(c_ctx)
    mods = _mods(cc, w_mod[0], row(b_mod[0])).reshape(n_rows, N_MOD, d)

    c64, s64 = _rope_tables(s)
    pad = lambda t: jnp.pad(t, ((0, 0), (0, LANES - D_ROPE)))
    ones_c, zeros_c = jnp.ones((lc, D_ROPE), F32), jnp.zeros((lc, D_ROPE), F32)
    common = (row(attn_norm[0]), win, wqt, wk, wvt, row(q_a_norm[0]), row(kv_a_norm[0]))
    kc, vct = _inproj(ctx, mods, lambda bi: b, *common, pad(ones_c), pad(zeros_c), ones_c.T, zeros_c.T,
                      latent=False, tm=min(TM_IN, lc))
    qt, kx, vxt, gb, u = _inproj(x, mods, lambda bi: bi, *common, pad(c64), pad(s64), c64.T, s64.T,
                                 latent=True, tm=tm_in)

    o_mla = _attention(qt, kc, vct, kx, vxt)

    h, xn, ri, rw, cnt = _post(o_mla, gb, u, x, mods, conv_w[0], row(o_norm_mla[0]), row(o_norm_conv[0]),
                               w_out[0].astype(BF16), row(ffn_norm[0]), wr,
                               router_bias[0].reshape(-1, 1), tm=tm_post)

    tmx = TM_EXPERT
    n_tiles = pl.cdiv(b * s * TOP_K, tmx) + N_EXPERTS
    counts = cnt[:, 0]
    tiles_per = (counts + tmx - 1) // tmx
    tile_end = jnp.cumsum(tiles_per)
    tile_start = tile_end - tiles_per
    base = (tile_start * tmx).astype(jnp.int32)
    tile_ids = jnp.arange(n_tiles, dtype=jnp.int32)
    te = jnp.minimum(jnp.sum(tile_end[None, :] <= tile_ids[:, None], axis=1), N_EXPERTS - 1)
    own = te[:, None] == jnp.arange(N_EXPERTS)[None, :]
    left = jnp.sum(jnp.where(own, counts[None, :] - (tile_ids[:, None] - tile_start[None, :]) * tmx, 0),
                   axis=1)
    tv = jnp.where(tile_ids < tile_end[-1], jnp.clip(left, 0, tmx), 0).astype(jnp.int32)
    tail = tile_end[-1] + jnp.arange(N_EXPERTS)
    zt = jnp.concatenate([jnp.where(tiles_per > 0, tile_end - 1, -1),
                          jnp.where(tail < n_tiles, tail, -1)]).astype(jnp.int32)

    pos = _positions(base, ri, tm=min(TM_POSITIONS, b * s))
    xs = _dispatch(zt, pos, xn, n_rows=n_tiles * tmx, tm=min(TM_DISPATCH, s), tile=tmx)
    ys = _experts(te.astype(jnp.int32), tv, xs, wgu, wd, tm=tmx)
    return _combine(pos, rw, xn, h, mods, wsgu, wsd, row(final_norm), ys, tm=min(TM_COMBINE, s))
```

```python
import functools
import math

import jax
import jax.numpy as jnp
import numpy as np
from jax import lax
from jax.experimental import pallas as pl
from jax.experimental.pallas import tpu as pltpu

F32 = jnp.float32
BF16 = jnp.bfloat16

N_HEADS = 8
D_NOPE = 128
D_ROPE = 64
D_V = 128
D_QK_PAD = 256
Q_RANK = 512
KV_RANK = 256
MLA_W = N_HEADS * D_V
GRID_W = 64
ROPE_BASE = 10000.0
ATTN_SCALE = (D_NOPE + D_ROPE) ** -0.5
LOG2E = math.log2(math.e)
N_EXPERTS = 32
N_GROUPS = 4
GROUP_SIZE = N_EXPERTS // N_GROUPS
TOPK_GROUPS = 2
TOP_K = 4
ROUTED_SCALE = 2.5
N_MOD = 6
EPS = 1e-6

C_Q = 0
C_KV = Q_RANK
C_KR = C_KV + KV_RANK
C_GB = C_KR + 128

LANES = 128
BF16_SUBLANES = 16
VMEM_LIMIT = 56 * 1024 * 1024

TM_IN = 512
ATTN_Q_TILES = 8
TM_POST = 512
TM_POSITIONS = 4096
TM_DISPATCH = 256
TM_EXPERT = 512
TM_COMBINE = 256
TN_MODS = 1024
DMA_UNROLL = 8
COMBINE_SLICES = 8


def _rms(x, g):
    return x * lax.rsqrt(jnp.mean(x * x, axis=-1, keepdims=True) + EPS) * g


def _silu(x):
    return x / (1.0 + jnp.exp(-x))


def _split_bf16(x):
    hi = x.astype(BF16)
    lo = (x - hi.astype(F32)).astype(BF16)
    return hi, lo


def _dot(a, b):
    return jnp.dot(a, b, preferred_element_type=F32)


def _mods_kernel(a_ref, w_ref, b_ref, o_ref):
    a = _silu(a_ref[...])
    ah, al = _split_bf16(a)
    wh, wl = _split_bf16(w_ref[...])
    o_ref[...] = _dot(ah, wh) + _dot(al, wh) + _dot(ah, wl) + b_ref[...]


def _mods(cc, w_mod, b_mod):
    rows, d = cc.shape
    n = w_mod.shape[1]
    return pl.pallas_call(
        _mods_kernel,
        grid=(n // TN_MODS,),
        in_specs=[pl.BlockSpec((rows, d), lambda j: (0, 0)),
                  pl.BlockSpec((d, TN_MODS), lambda j: (0, j)),
                  pl.BlockSpec((1, TN_MODS), lambda j: (0, j))],
        out_specs=pl.BlockSpec((rows, TN_MODS), lambda j: (0, j)),
        out_shape=jax.ShapeDtypeStruct((rows, n), F32),
        compiler_params=pltpu.CompilerParams(dimension_semantics=("arbitrary",),
                                             vmem_limit_bytes=VMEM_LIMIT),
        name="mods",
    )(cc, w_mod, b_mod)


def _inproj_kernel(x_ref, mods_ref, g_ref, win_ref, wqt_ref, wk_ref, wvt_ref, qg_ref, kvg_ref,
                   ca_ref, sa_ref, ct_ref, st_ref, *out_refs, latent):
    x = x_ref[0]
    shift = mods_ref[0, 0:1, :]
    scale = mods_ref[0, 1:2, :]
    nx = (_rms(x, g_ref[...]) * (1.0 + scale) + shift).astype(BF16)

    if latent:
        qt_ref, k_ref, vt_ref, gb_ref, u_ref = out_refs
        q_a = _dot(nx, win_ref[:, C_Q:C_KV])
        qnt = _rms(q_a, qg_ref[...]).T.astype(BF16)
        qt = _dot(wqt_ref[...], qnt) * (ATTN_SCALE * LOG2E)
        ct = ct_ref[...]
        st = st_ref[...]
        for h in range(N_HEADS):
            r0 = h * D_QK_PAD
            r1 = r0 + D_NOPE
            r2 = r1 + D_ROPE
            qt_ref[0, 0, r0:r1, :] = qt[r0:r1].astype(BF16)
            qt_ref[0, 0, r1:r2, :] = (qt[r1:r2] * ct + qt[r2:r0 + D_QK_PAD] * st).astype(BF16)
            qt_ref[0, 0, r2:r0 + D_QK_PAD, :] = jnp.zeros((D_ROPE, qt.shape[1]), BF16)
        gb_ref[0] = _dot(nx, win_ref[:, C_GB:C_GB + MLA_W]).astype(BF16)
        g_c = _dot(nx, win_ref[:, C_GB + MLA_W:C_GB + 2 * MLA_W])
        hh = _dot(nx, win_ref[:, C_GB + 2 * MLA_W:C_GB + 3 * MLA_W])
        u_ref[0] = (g_c * hh).astype(BF16)
    else:
        k_ref, vt_ref = out_refs

    kv_a = _dot(nx, win_ref[:, C_KV:C_KR])
    kvn = _rms(kv_a, kvg_ref[...])
    kn = _dot(kvn.astype(BF16), wk_ref[...])
    vt_ref[0, 0] = _dot(wvt_ref[...], kvn.T.astype(BF16)).astype(BF16)
    blk = _dot(nx, win_ref[:, C_KR:C_GB])
    kr = (blk * ca_ref[...] + pltpu.roll(blk, D_ROPE, axis=1) * sa_ref[...]).astype(BF16)
    for h in range(N_HEADS):
        c0 = h * D_QK_PAD
        k_ref[0, :, c0:c0 + D_NOPE] = kn[:, h * D_NOPE:(h + 1) * D_NOPE].astype(BF16)
        k_ref[0, :, c0 + D_NOPE:c0 + D_QK_PAD] = kr


def _inproj(x, mods, mod_row, g, win, wqt, wk, wvt, qg, kvg, ca, sa, ct, st, *, latent, tm):
    b, s, d = x.shape
    nt = s // tm
    const = lambda a: pl.BlockSpec(a.shape, lambda bi, i: (0,) * a.ndim, pipeline_mode=pl.Buffered(1))
    tok = lambda w: pl.BlockSpec((1, tm, w), lambda bi, i: (bi, i, 0))
    tok_t = lambda w: pl.BlockSpec((1, 1, w, tm), lambda bi, i: (bi, i, 0, 0))
    in_specs = [tok(d),
                pl.BlockSpec((1, N_MOD, d), lambda bi, i: (mod_row(bi), 0, 0)),
                const(g), const(win), const(wqt), const(wk), const(wvt), const(qg), const(kvg),
                pl.BlockSpec((tm, LANES), lambda bi, i: (i, 0)),
                pl.BlockSpec((tm, LANES), lambda bi, i: (i, 0)),
                pl.BlockSpec((D_ROPE, tm), lambda bi, i: (0, i)),
                pl.BlockSpec((D_ROPE, tm), lambda bi, i: (0, i))]
    kq = N_HEADS * D_QK_PAD
    row_major = lambda w: (tok(w), jax.ShapeDtypeStruct((b, s, w), BF16))
    col_major = lambda w: (tok_t(w), jax.ShapeDtypeStruct((b, nt, w, tm), BF16))
    outs = [row_major(kq), col_major(MLA_W)]
    if latent:
        outs = [col_major(kq)] + outs + [row_major(MLA_W), row_major(MLA_W)]
    return pl.pallas_call(
        functools.partial(_inproj_kernel, latent=latent),
        grid=(b, nt),
        in_specs=in_specs,
        out_specs=[o[0] for o in outs],
        out_shape=[o[1] for o in outs],
        compiler_params=pltpu.CompilerParams(dimension_semantics=("parallel", "arbitrary"),
                                             vmem_limit_bytes=VMEM_LIMIT),
        name="inproj_latent" if latent else "inproj_ctx",
    )(x, mods, g, win, wqt, wk, wvt, qg, kvg, ca, sa, ct, st)


def _attn_kernel(qt_ref, kc_ref, vct_ref, kx_ref, vxt_ref, o_ref, s_ref):
    qt = jnp.concatenate([qt_ref[0, j] for j in range(qt_ref.shape[1])], axis=1)
    tq = qt.shape[1]
    nb = vxt_ref.shape[1]
    tk = kx_ref.shape[1] // nb
    lc = kc_ref.shape[1]

    def scores(k):
        s = _dot(k, qt)
        return s, jnp.max(s, axis=0, keepdims=True)

    def keys(j):
        return kx_ref[0, pl.ds(pl.multiple_of(j * tk, tk), tk), :]

    def absorb(s, mx, vt, carry):
        m, l, acc = carry
        m_new = jnp.maximum(m, mx)
        alpha = jnp.exp2(m - m_new)
        p = jnp.exp2(s - m_new)
        l = alpha * l + jnp.sum(p, axis=0, keepdims=True)
        acc = alpha * acc + _dot(vt, p.astype(BF16))
        return m_new, l, acc

    def pair(cur_rows, cur_vt, j, carry, mx):
        s_n, mx_n = scores(keys(j))
        s_ref[1] = s_n
        carry = absorb(s_ref[0, :cur_rows, :], mx, cur_vt, carry)
        s_a, mx_a = scores(keys(j + 1))
        s_ref[0] = s_a
        carry = absorb(s_ref[1], mx_n, vxt_ref[0, j], carry)
        return carry, mx_a

    init = (jnp.full((1, tq), -jnp.inf, F32), jnp.zeros((1, tq), F32), jnp.zeros((D_V, tq), F32))
    s, mx = scores(kc_ref[0])
    s_ref[0, :lc, :] = s
    state = pair(lc, vct_ref[0, 0], 0, init, mx)

    def body(i, state):
        return pair(tk, vxt_ref[0, 2 * i - 1], 2 * i, *state)

    carry, mx = lax.fori_loop(1, nb // 2, body, state)
    _, l, acc = absorb(s_ref[0], mx, vxt_ref[0, nb - 1], carry)
    o_ref[0] = (acc / l).T.astype(o_ref.dtype)


def _attention(qt, kc, vct, kx, vxt):
    b, nt, _, tk = qt.shape
    s = kx.shape[1]
    lc = kc.shape[1]
    nq = min(ATTN_Q_TILES, nt)
    tq = nq * tk
    assert nt % 2 == 0 and nt % nq == 0 and lc <= tk, \
        "latent chunks are absorbed in pairs after one context chunk"
    return pl.pallas_call(
        _attn_kernel,
        grid=(b, N_HEADS, nt // nq),
        in_specs=[pl.BlockSpec((1, nq, D_QK_PAD, tk), lambda bi, h, i: (bi, i, h, 0)),
                  pl.BlockSpec((1, lc, D_QK_PAD), lambda bi, h, i: (bi, 0, h)),
                  pl.BlockSpec((1, 1, D_V, lc), lambda bi, h, i: (bi, 0, h, 0)),
                  pl.BlockSpec((1, s, D_QK_PAD), lambda bi, h, i: (bi, 0, h)),
                  pl.BlockSpec((1, nt, D_V, tk), lambda bi, h, i: (bi, 0, h, 0))],
        out_specs=pl.BlockSpec((1, tq, D_V), lambda bi, h, i: (bi, i, h)),
        out_shape=jax.ShapeDtypeStruct((b, s, MLA_W), BF16),
        scratch_shapes=[pltpu.VMEM((2, tk, tq), F32)],
        compiler_params=pltpu.CompilerParams(
            dimension_semantics=("parallel", "parallel", "arbitrary"), vmem_limit_bytes=VMEM_LIMIT),
        name="attn",
    )(qt, kc, vct, kx, vxt)


def _route(logits_t, bias):
    e, tm = logits_t.shape
    neg = -jnp.inf
    s = 1.0 / (1.0 + jnp.exp(-logits_t))
    sel = s + bias
    io_g = lax.broadcasted_iota(jnp.int32, (GROUP_SIZE, tm), 0)
    gs = []
    for g in range(N_GROUPS):
        blk = sel[g * GROUP_SIZE:(g + 1) * GROUP_SIZE]
        m1 = jnp.max(blk, axis=0, keepdims=True)
        i1 = jnp.min(jnp.where(blk == m1, io_g, GROUP_SIZE), axis=0, keepdims=True)
        m2 = jnp.max(jnp.where(io_g == i1, neg, blk), axis=0, keepdims=True)
        gs.append(m1 + m2)
    masks = []
    for g in range(N_GROUPS):
        rank = jnp.zeros((1, tm), jnp.int32)
        for o in range(N_GROUPS):
            if o == g:
                continue
            ahead = (gs[o] > gs[g]) | (gs[o] == gs[g]) if o < g else (gs[o] > gs[g])
            rank = rank + ahead.astype(jnp.int32)
        masks.append(jnp.broadcast_to(rank < TOPK_GROUPS, (GROUP_SIZE, tm)))
    emask = jnp.concatenate(masks, axis=0)
    cur = jnp.where(emask, sel, neg)
    io_e = lax.broadcasted_iota(jnp.int32, (e, tm), 0)
    chosen = jnp.zeros((e, tm), jnp.bool_)
    slots = []
    for _ in range(TOP_K):
        m = jnp.max(cur, axis=0, keepdims=True)
        i = jnp.min(jnp.where(cur == m, io_e, e), axis=0, keepdims=True)
        hit = io_e == i
        chosen = chosen | hit
        cur = jnp.where(hit, neg, cur)
        slots.append((hit, i))
    wsum = jnp.sum(jnp.where(chosen, s, 0.0), axis=0, keepdims=True)
    return chosen, jnp.where(chosen, s / wsum * ROUTED_SCALE, 0.0), slots


def _post_kernel(om_ref, gb_ref, u_ref, up_ref, un_ref, x_ref, mods_ref, cw_ref, g1_ref, g2_ref,
                 wout_ref, gf_ref, wr_ref, rb_ref, tri_ref,
                 h_ref, xn_ref, ri_ref, rw_ref, cnt_ref, run_ref):
    first = (pl.program_id(0) == 0) & (pl.program_id(1) == 0)

    @pl.when(first)
    def _():
        run_ref[...] = jnp.zeros_like(run_ref)

    i = pl.program_id(1)
    nt = pl.num_programs(1)
    tm = u_ref.shape[1]
    u = u_ref[0].astype(F32)
    prev = jnp.where(i > 0, up_ref[0].astype(F32)[BF16_SUBLANES - 1:BF16_SUBLANES], 0.0)
    nxt = jnp.where(i < nt - 1, un_ref[0].astype(F32)[0:1], 0.0)
    row = lax.broadcasted_iota(jnp.int32, u.shape, 0)
    u_m1 = jnp.where(row == 0, prev, pltpu.roll(u, 1, axis=0))
    u_p1 = jnp.where(row == tm - 1, nxt, pltpu.roll(u, tm - 1, axis=0))
    cw = cw_ref[...]
    y = gb_ref[0].astype(F32) * (cw[0:1] * u_m1 + cw[1:2] * u + cw[2:3] * u_p1)
    o = jnp.concatenate([_rms(om_ref[0].astype(F32), g1_ref[...]).astype(BF16),
                         _rms(y, g2_ref[...]).astype(BF16)], axis=-1)
    ga = mods_ref[0, 2:3, :]
    sf = mods_ref[0, 3:4, :]
    scf = mods_ref[0, 4:5, :]
    h = x_ref[0] + ga * _dot(o, wout_ref[...])
    h_ref[0] = h
    xn = _rms(h, gf_ref[...]) * (1.0 + scf) + sf
    xh, xl = _split_bf16(xn)
    xn_ref[...] = xn
    hh_hl = _dot(xh, wr_ref[...])
    logits = hh_hl[:, :LANES] + hh_hl[:, LANES:] + _dot(xl, wr_ref[:, :LANES])
    chosen, comb_t, slots = _route(logits.T[:N_EXPERTS], rb_ref[...])
    sel = jnp.where(chosen, 1.0, 0.0)
    before = _dot(sel.astype(BF16), tri_ref[...]) + run_ref[...]
    run_ref[...] += jnp.sum(sel, axis=1, keepdims=True)
    cnt_ref[...] = jnp.broadcast_to(run_ref[...], cnt_ref.shape).astype(jnp.int32)
    pick = lambda hit, v: jnp.sum(jnp.where(hit, v, 0.0), axis=0, keepdims=True)
    ri_ref[...] = jnp.concatenate(
        [idx for _, idx in slots] + [pick(hit, before).astype(jnp.int32) for hit, _ in slots], axis=0)
    rw_ref[...] = jnp.concatenate(
        [pick(hit, comb_t) for hit, _ in slots] + [jnp.zeros((LANES - TOP_K, tm), F32)], axis=0).T


def _post(o_mla, gb, u, x, mods, conv_w, g1, g2, wout, gf, wr, rb, *, tm):
    b, s, d = x.shape
    nt = s // tm
    hb = tm // BF16_SUBLANES
    last = s // BF16_SUBLANES - 1
    tri = jnp.asarray(np.triu(np.ones((tm, tm), np.float32), k=1), BF16)
    const = lambda a: pl.BlockSpec(a.shape, lambda bi, i: (0,) * a.ndim, pipeline_mode=pl.Buffered(1))
    tok = lambda w: pl.BlockSpec((1, tm, w), lambda bi, i: (bi, i, 0))
    flat = lambda w: pl.BlockSpec((tm, w), lambda bi, i: (bi * nt + i, 0))
    in_specs = [tok(MLA_W), tok(MLA_W), tok(MLA_W),
                pl.BlockSpec((1, BF16_SUBLANES, MLA_W), lambda bi, i: (bi, jnp.maximum(i * hb - 1, 0), 0)),
                pl.BlockSpec((1, BF16_SUBLANES, MLA_W), lambda bi, i: (bi, jnp.minimum((i + 1) * hb, last), 0)),
                tok(d),
                pl.BlockSpec((1, N_MOD, d), lambda bi, i: (bi, 0, 0)),
                const(conv_w), const(g1), const(g2), const(wout), const(gf), const(wr),
                const(rb), const(tri)]
    t = b * s
    return pl.pallas_call(
        _post_kernel,
        grid=(b, nt),
        in_specs=in_specs,
        out_specs=[tok(d), flat(d),
                   pl.BlockSpec((2 * TOP_K, tm), lambda bi, i: (0, bi * nt + i)),
                   flat(LANES),
                   pl.BlockSpec((N_EXPERTS, LANES), lambda bi, i: (0, 0))],
        out_shape=[jax.ShapeDtypeStruct((b, s, d), F32),
                   jax.ShapeDtypeStruct((t, d), F32),
                   jax.ShapeDtypeStruct((2 * TOP_K, t), jnp.int32),
                   jax.ShapeDtypeStruct((t, LANES), F32),
                   jax.ShapeDtypeStruct((N_EXPERTS, LANES), jnp.int32)],
        scratch_shapes=[pltpu.VMEM((N_EXPERTS, 1), F32)],
        compiler_params=pltpu.CompilerParams(dimension_semantics=("arbitrary", "arbitrary"),
                                             vmem_limit_bytes=VMEM_LIMIT),
        name="post",
    )(o_mla, gb, u, u, u, x, mods, conv_w, g1, g2, wout, gf, wr, rb, tri)


def _positions_kernel(base_ref, ri_ref, pos_ref):
    e = ri_ref[0:TOP_K, :]
    pos = ri_ref[TOP_K:2 * TOP_K, :]
    for j in range(N_EXPERTS):
        pos = pos + jnp.where(e == j, base_ref[j], 0)
    pos_ref[...] = pos


def _positions(base, ri, *, tm):
    t = ri.shape[1]
    return pl.pallas_call(
        _positions_kernel,
        grid_spec=pltpu.PrefetchScalarGridSpec(
            num_scalar_prefetch=1,
            grid=(t // tm,),
            in_specs=[pl.BlockSpec((2 * TOP_K, tm), lambda i, base: (0, i))],
            out_specs=pl.BlockSpec((TOP_K, tm), lambda i, base: (0, i))),
        out_shape=jax.ShapeDtypeStruct((TOP_K, t), jnp.int32),
        compiler_params=pltpu.CompilerParams(dimension_semantics=("arbitrary",)),
        name="positions",
    )(base, ri)


def _dispatch_kernel(zt_ref, pos_ref, xn_ref, xs_ref, zero_ref, sem):
    tm = xn_ref.shape[0]
    tz = zero_ref.shape[0]

    @pl.when(pl.program_id(0) == 0)
    def _():
        zero_ref[...] = jnp.zeros_like(zero_ref)
        n_fill = zt_ref.shape[0]
        fill = lambda j: pltpu.make_async_copy(
            zero_ref, xs_ref.at[pl.ds(jnp.maximum(zt_ref[j], 0) * tz, tz)], sem)
        for j in range(n_fill):
            pl.when(zt_ref[j] >= 0)(fill(j).start)
        for j in range(n_fill):
            pl.when(zt_ref[j] >= 0)(fill(j).wait)

    for t in range(tm):
        for k in range(TOP_K):
            pltpu.make_async_copy(xn_ref.at[pl.ds(t, 1)], xs_ref.at[pl.ds(pos_ref[k, t], 1)],
                                  sem).start(priority=k % 2)
    for _ in range(TOP_K):
        pltpu.make_async_copy(xn_ref, xs_ref.at[pl.ds(0, tm)], sem).wait()


def _dispatch(zt, pos, xn, *, n_rows, tm, tile):
    t, w = xn.shape
    return pl.pallas_call(
        _dispatch_kernel,
        grid_spec=pltpu.PrefetchScalarGridSpec(
            num_scalar_prefetch=1,
            grid=(t // tm,),
            in_specs=[pl.BlockSpec((TOP_K, tm), lambda i, zt: (0, i), memory_space=pltpu.SMEM),
                      pl.BlockSpec((tm, w), lambda i, zt: (i, 0))],
            out_specs=pl.BlockSpec(memory_space=pl.ANY),
            scratch_shapes=[pltpu.VMEM((tile, w), xn.dtype), pltpu.SemaphoreType.DMA(())]),
        out_shape=jax.ShapeDtypeStruct((n_rows, w), xn.dtype),
        compiler_params=pltpu.CompilerParams(dimension_semantics=("arbitrary",),
                                             vmem_limit_bytes=VMEM_LIMIT),
        name="dispatch",
    )(zt, pos, xn)


def _expert_kernel(te_ref, tv_ref, xs_ref, wgu_ref, wd_ref, ys_ref):
    nv = tv_ref[pl.program_id(0)]
    ff = wd_ref.shape[1]

    @pl.when(nv > 0)
    def _():
        rows = lax.broadcasted_iota(jnp.int32, xs_ref.shape, 0)
        x = jnp.where(rows < nv, xs_ref[...], 0.0).astype(BF16)
        gu = _dot(x, wgu_ref[0])
        a = (_silu(gu[:, :ff]) * gu[:, ff:]).astype(BF16)
        ys_ref[...] = _dot(a, wd_ref[0])

    @pl.when(nv == 0)
    def _():
        ys_ref[...] = jnp.zeros_like(ys_ref)


def _experts(te, tv, xs, wgu, wd, *, tm):
    n_rows, w = xs.shape
    d = wd.shape[2]
    return pl.pallas_call(
        _expert_kernel,
        grid_spec=pltpu.PrefetchScalarGridSpec(
            num_scalar_prefetch=2,
            grid=(n_rows // tm,),
            in_specs=[pl.BlockSpec((tm, w), lambda i, te, tv: (i, 0)),
                      pl.BlockSpec((1,) + wgu.shape[1:], lambda i, te, tv: (te[i], 0, 0)),
                      pl.BlockSpec((1,) + wd.shape[1:], lambda i, te, tv: (te[i], 0, 0))],
            out_specs=pl.BlockSpec((tm, d), lambda i, te, tv: (i, 0))),
        out_shape=jax.ShapeDtypeStruct((n_rows, d), F32),
        compiler_params=pltpu.CompilerParams(dimension_semantics=("arbitrary",),
                                             vmem_limit_bytes=VMEM_LIMIT),
        name="experts",
    )(te, tv, xs, wgu, wd)


def _combine_kernel(pos_ref, pos_next_ref, rw_ref, xn_ref, h_ref, mods_ref, wsgu_ref, wsd_ref, gfin_ref,
                    ys_ref, o_ref, buf_ref, sems):
    tm, d = xn_ref.shape
    g = pl.program_id(0) * pl.num_programs(1) + pl.program_id(1)
    n = pl.num_programs(0) * pl.num_programs(1)

    def start_rows(p_ref, t, to):
        for k in range(TOP_K):
            pltpu.make_async_copy(ys_ref.at[pl.ds(p_ref[k, t], 1)],
                                  buf_ref.at[to, k, pl.ds(t, 1)], sems.at[to]).start(priority=k % 2)

    def wait_rows(to):
        for k in range(TOP_K):
            pltpu.make_async_copy(ys_ref.at[pl.ds(0, tm)], buf_ref.at[to, k], sems.at[to]).wait()

    @pl.when(g == 0)
    def _():
        def issue(t, carry):
            start_rows(pos_ref, t, 0)
            return carry
        lax.fori_loop(0, tm, issue, 0, unroll=DMA_UNROLL)

    def tile(slot):
        other = 1 - slot
        xb = xn_ref[...].astype(BF16)
        kc, tc = d // COMBINE_SLICES, tm // COMBINE_SLICES
        sgu = None
        for j in range(COMBINE_SLICES):
            part = _dot(xb[:, j * kc:(j + 1) * kc], wsgu_ref[j * kc:(j + 1) * kc, :])
            sgu = part if sgu is None else sgu + part
            for t in range(j * tc, (j + 1) * tc):
                start_rows(pos_next_ref, t, other)
        sf = sgu.shape[1] // 2
        y = _dot((_silu(sgu[:, :sf]) * sgu[:, sf:]).astype(BF16), wsd_ref[...])

        wait_rows(slot)
        rw = rw_ref[...]
        for k in range(TOP_K):
            y = y + rw[:, k:k + 1] * buf_ref[slot, k]
        gate = mods_ref[0, 5:6, :]
        o_ref[0] = _rms(h_ref[0] + gate * y, gfin_ref[...])

        @pl.when(g == n - 1)
        def _():
            wait_rows(other)

    pl.when(g % 2 == 0)(functools.partial(tile, 0))
    pl.when(g % 2 == 1)(functools.partial(tile, 1))


def _combine(pos, rw, xn, h, mods, wsgu, wsd, gfin, ys, *, tm):
    b, s, d = h.shape
    nt = s // tm
    last = b * nt - 1
    const = lambda a: pl.BlockSpec(a.shape, lambda bi, i: (0,) * a.ndim, pipeline_mode=pl.Buffered(1))
    flat = lambda w: pl.BlockSpec((tm, w), lambda bi, i: (bi * nt + i, 0))
    return pl.pallas_call(
        _combine_kernel,
        grid=(b, nt),
        in_specs=[pl.BlockSpec((TOP_K, tm), lambda bi, i: (0, bi * nt + i), memory_space=pltpu.SMEM),
                  pl.BlockSpec((TOP_K, tm), lambda bi, i: (0, jnp.minimum(bi * nt + i + 1, last)),
                               memory_space=pltpu.SMEM),
                  flat(LANES), flat(xn.shape[1]),
                  pl.BlockSpec((1, tm, d), lambda bi, i: (bi, i, 0)),
                  pl.BlockSpec((1, N_MOD, d), lambda bi, i: (bi, 0, 0)),
                  const(wsgu), const(wsd), const(gfin),
                  pl.BlockSpec(memory_space=pl.ANY)],
        out_specs=pl.BlockSpec((1, tm, d), lambda bi, i: (bi, i, 0)),
        out_shape=jax.ShapeDtypeStruct((b, s, d), F32),
        scratch_shapes=[pltpu.VMEM((2, TOP_K, tm, d), F32), pltpu.SemaphoreType.DMA((2,))],
        compiler_params=pltpu.CompilerParams(dimension_semantics=("arbitrary", "arbitrary"),
                                             vmem_limit_bytes=VMEM_LIMIT),
        name="combine",
    )(pos, pos, rw, xn, h, mods, wsgu, wsd, gfin, ys)


def _rope_tables(s):
    nf = D_ROPE // 4
    pos = np.arange(s)
    inv = ROPE_BASE ** (-np.arange(nf, dtype=np.float64) / nf)
    ar = (pos // GRID_W)[:, None] * inv
    ac = (pos % GRID_W)[:, None] * inv
    c64 = np.concatenate([np.cos(ar), np.cos(ar), np.cos(ac), np.cos(ac)], axis=-1)
    s64 = np.concatenate([-np.sin(ar), np.sin(ar), -np.sin(ac), np.sin(ac)], axis=-1)
    return c64.astype(np.float32), s64.astype(np.float32)


def _swap_pairs(w):
    nf = D_ROPE // 4
    return jnp.concatenate([w[..., nf:2 * nf], w[..., :nf], w[..., 3 * nf:], w[..., 2 * nf:3 * nf]], axis=-1)


def kernel(x, c, ctx, c_ctx, w_mod, b_mod, attn_norm, w_in, q_a_norm, w_q_b, kv_a_norm, w_kv_b, conv_w,
           o_norm_mla, o_norm_conv, w_out, ffn_norm, w_router, router_bias, w_exp_gate, w_exp_up,
           w_exp_down, w_sh_gate, w_sh_up, w_sh_down, final_norm):
    b, s, d = x.shape
    lc = ctx.shape[1]
    assert w_mod.shape[0] == 1, "single trunk layer"
    assert s % GRID_W == 0
    assert lc <= TM_IN, "context keys are processed as a single attention chunk"
    tm_in, tm_post = min(TM_IN, s), min(TM_POST, s)

    wi = w_in[0]
    kr_cols = wi[:, C_KR:C_KR + D_ROPE]
    win = jnp.concatenate([wi[:, :C_KR], kr_cols, _swap_pairs(kr_cols), wi[:, C_KR + D_ROPE:]],
                          axis=-1).astype(BF16)
    wq3 = w_q_b[0].reshape(Q_RANK, N_HEADS, D_NOPE + D_ROPE)
    wq = jnp.concatenate([wq3, _swap_pairs(wq3[..., D_NOPE:])], axis=-1)
    wqt = wq.reshape(Q_RANK, N_HEADS * D_QK_PAD).T.astype(BF16)
    wkv3 = w_kv_b[0].reshape(KV_RANK, N_HEADS, D_NOPE + D_V)
    wk = wkv3[..., :D_NOPE].reshape(KV_RANK, MLA_W).astype(BF16)
    wvt = wkv3[..., D_NOPE:].reshape(KV_RANK, MLA_W).T.astype(BF16)
    wr = jnp.pad(w_router[0], ((0, 0), (0, LANES - N_EXPERTS)))
    wr = jnp.concatenate(_split_bf16(wr), axis=-1)
    wgu = jnp.concatenate([w_exp_gate[0], w_exp_up[0]], axis=-1).astype(BF16)
    wd = w_exp_down[0].astype(BF16)
    wsgu = jnp.concatenate([w_sh_gate[0], w_sh_up[0]], axis=-1).astype(BF16)
    wsd = w_sh_down[0].astype(BF16)
    row = lambda v: v.reshape(1, -1)

    n_rows = -(-(b + 1) // 8) * 8
    cc = jnp.zeros((n_rows, d), F32).at[:b].set(c).at[b].set(c_ctx)
    mods = _mods(cc, w_mod[0], row(b_mod[0])).reshape(n_rows, N_MOD, d)

    c64, s64 = _rope_tables(s)
    pad = lambda t: np.pad(t, ((0, 0), (0, LANES - D_ROPE)))
    ones_c, zeros_c = np.ones((lc, D_ROPE), np.float32), np.zeros((lc, D_ROPE), np.float32)
    common = (row(attn_norm[0]), win, wqt, wk, wvt, row(q_a_norm[0]), row(kv_a_norm[0]))
    kc, vct = _inproj(ctx, mods, lambda bi: b, *common, pad(ones_c), pad(zeros_c), ones_c.T, zeros_c.T,
                      latent=False, tm=min(TM_IN, lc))
    qt, kx, vxt, gb, u = _inproj(x, mods, lambda bi: bi, *common, pad(c64), pad(s64), c64.T, s64.T,
                                 latent=True, tm=tm_in)

    o_mla = _attention(qt, kc, vct, kx, vxt)

    h, xn, ri, rw, cnt = _post(o_mla, gb, u, x, mods, conv_w[0], row(o_norm_mla[0]), row(o_norm_conv[0]),
                               w_out[0].astype(BF16), row(ffn_norm[0]), wr,
                               router_bias[0].reshape(-1, 1), tm=tm_post)

    tmx = TM_EXPERT
    n_tiles = pl.cdiv(b * s * TOP_K, tmx) + N_EXPERTS
    counts = cnt[:, 0]
    tiles_per = (counts + tmx - 1) // tmx
    tile_end = jnp.cumsum(tiles_per)
    tile_start = tile_end - tiles_per
    base = (tile_start * tmx).astype(jnp.int32)
    tile_ids = jnp.arange(n_tiles, dtype=jnp.int32)
    te = jnp.minimum(jnp.sum(tile_end[None, :] <= tile_ids[:, None], axis=1), N_EXPERTS - 1)
    own = te[:, None] == jnp.arange(N_EXPERTS)[None, :]
    left = jnp.sum(jnp.where(own, counts[None, :] - (tile_ids[:, None] - tile_start[None, :]) * tmx, 0),
                   axis=1)
    tv = jnp.where(tile_ids < tile_end[-1], jnp.clip(left, 0, tmx), 0).astype(jnp.int32)
    tail = tile_end[-1] + jnp.arange(N_EXPERTS)
    zt = jnp.concatenate([jnp.where(tiles_per > 0, tile_end - 1, -1),
                          jnp.where(tail < n_tiles, tail, -1)]).astype(jnp.int32)

    pos = _positions(base, ri, tm=min(TM_POSITIONS, b * s))
    xs = _dispatch(zt, pos, xn, n_rows=n_tiles * tmx, tm=min(TM_DISPATCH, s), tile=tmx)
    ys = _experts(te.astype(jnp.int32), tv, xs, wgu, wd, tm=tmx)
    return _combine(pos, rw, xn, h, mods, wsgu, wsd, row(final_norm), ys, tm=min(TM_COMBINE, s))
```

```python
import functools
import math

import jax
import jax.numpy as jnp
import numpy as np
from jax import lax
from jax.experimental import pallas as pl
from jax.experimental.pallas import tpu as pltpu

F32 = jnp.float32
BF16 = jnp.bfloat16

N_HEADS = 8
D_NOPE = 128
D_ROPE = 64
D_V = 128
D_QK_PAD = 256
Q_RANK = 512
KV_RANK = 256
MLA_W = N_HEADS * D_V
GRID_W = 64
ROPE_BASE = 10000.0
ATTN_SCALE = (D_NOPE + D_ROPE) ** -0.5
LOG2E = math.log2(math.e)
N_EXPERTS = 32
N_GROUPS = 4
GROUP_SIZE = N_EXPERTS // N_GROUPS
TOPK_GROUPS = 2
TOP_K = 4
ROUTED_SCALE = 2.5
N_MOD = 6
EPS = 1e-6

C_Q = 0
C_KV = Q_RANK
C_KR = C_KV + KV_RANK
C_GB = C_KR + 128

LANES = 128
BF16_SUBLANES = 16
VMEM_LIMIT = 56 * 1024 * 1024

TM_IN = 512
ATTN_Q_TILES = 8
TM_POST = 512
TM_POSITIONS = 4096
TM_DISPATCH = 256
TM_EXPERT = 512
TM_COMBINE = 256
TN_MODS = 1024
DMA_UNROLL = 8
COMBINE_SLICES = 8


def _rms(x, g):
    return x * lax.rsqrt(jnp.mean(x * x, axis=-1, keepdims=True) + EPS) * g


def _silu(x):
    return x / (1.0 + jnp.exp(-x))


def _split_bf16(x):
    hi = x.astype(BF16)
    lo = (x - hi.astype(F32)).astype(BF16)
    return hi, lo


def _dot(a, b):
    return jnp.dot(a, b, preferred_element_type=F32)


def _mods_kernel(a_ref, w_ref, b_ref, o_ref):
    a = _silu(a_ref[...])
    ah, al = _split_bf16(a)
    wh, wl = _split_bf16(w_ref[...])
    o_ref[...] = _dot(ah, wh) + _dot(al, wh) + _dot(ah, wl) + b_ref[...]


def _mods(cc, w_mod, b_mod):
    rows, d = cc.shape
    n = w_mod.shape[1]
    return pl.pallas_call(
        _mods_kernel,
        grid=(n // TN_MODS,),
        in_specs=[pl.BlockSpec((rows, d), lambda j: (0, 0)),
                  pl.BlockSpec((d, TN_MODS), lambda j: (0, j)),
                  pl.BlockSpec((1, TN_MODS), lambda j: (0, j))],
        out_specs=pl.BlockSpec((rows, TN_MODS), lambda j: (0, j)),
        out_shape=jax.ShapeDtypeStruct((rows, n), F32),
        compiler_params=pltpu.CompilerParams(dimension_semantics=("arbitrary",),
                                             vmem_limit_bytes=VMEM_LIMIT),
        name="mods",
    )(cc, w_mod, b_mod)


def _inproj_kernel(x_ref, mods_ref, g_ref, win_ref, wqt_ref, wk_ref, wvt_ref, qg_ref, kvg_ref,
                   ca_ref, sa_ref, ct_ref, st_ref, *out_refs, latent):
    x = x_ref[0]
    shift = mods_ref[0, 0:1, :]
    scale = mods_ref[0, 1:2, :]
    nx = (_rms(x, g_ref[...]) * (1.0 + scale) + shift).astype(BF16)

    if latent:
        qt_ref, k_ref, vt_ref, gb_ref, u_ref = out_refs
        q_a = _dot(nx, win_ref[:, C_Q:C_KV])
        qnt = _rms(q_a, qg_ref[...]).T.astype(BF16)
        qt = _dot(wqt_ref[...], qnt) * (ATTN_SCALE * LOG2E)
        ct = ct_ref[...]
        st = st_ref[...]
        for h in range(N_HEADS):
            r0 = h * D_QK_PAD
            r1 = r0 + D_NOPE
            r2 = r1 + D_ROPE
            qt_ref[0, 0, r0:r1, :] = qt[r0:r1].astype(BF16)
            qt_ref[0, 0, r1:r2, :] = (qt[r1:r2] * ct + qt[r2:r0 + D_QK_PAD] * st).astype(BF16)
            qt_ref[0, 0, r2:r0 + D_QK_PAD, :] = jnp.zeros((D_ROPE, qt.shape[1]), BF16)
        gb_ref[0] = _dot(nx, win_ref[:, C_GB:C_GB + MLA_W]).astype(BF16)
        g_c = _dot(nx, win_ref[:, C_GB + MLA_W:C_GB + 2 * MLA_W])
        hh = _dot(nx, win_ref[:, C_GB + 2 * MLA_W:C_GB + 3 * MLA_W])
        u_ref[0] = (g_c * hh).astype(BF16)
    else:
        k_ref, vt_ref = out_refs

    kv_a = _dot(nx, win_ref[:, C_KV:C_KR])
    kvn = _rms(kv_a, kvg_ref[...])
    kn = _dot(kvn.astype(BF16), wk_ref[...])
    vt_ref[0, 0] = _dot(wvt_ref[...], kvn.T.astype(BF16)).astype(BF16)
    blk = _dot(nx, win_ref[:, C_KR:C_GB])
    kr = (blk * ca_ref[...] + pltpu.roll(blk, D_ROPE, axis=1) * sa_ref[...]).astype(BF16)
    for h in range(N_HEADS):
        c0 = h * D_QK_PAD
        k_ref[0, :, c0:c0 + D_NOPE] = kn[:, h * D_NOPE:(h + 1) * D_NOPE].astype(BF16)
        k_ref[0, :, c0 + D_NOPE:c0 + D_QK_PAD] = kr


def _inproj(x, mods, mod_row, g, win, wqt, wk, wvt, qg, kvg, ca, sa, ct, st, *, latent, tm):
    b, s, d = x.shape
    nt = s // tm
    const = lambda a: pl.BlockSpec(a.shape, lambda bi, i: (0,) * a.ndim, pipeline_mode=pl.Buffered(1))
    tok = lambda w: pl.BlockSpec((1, tm, w), lambda bi, i: (bi, i, 0))
    tok_t = lambda w: pl.BlockSpec((1, 1, w, tm), lambda bi, i: (bi, i, 0, 0))
    in_specs = [tok(d),
                pl.BlockSpec((1, N_MOD, d), lambda bi, i: (mod_row(bi), 0, 0)),
                const(g), const(win), const(wqt), const(wk), const(wvt), const(qg), const(kvg),
                pl.BlockSpec((tm, LANES), lambda bi, i: (i, 0)),
                pl.BlockSpec((tm, LANES), lambda bi, i: (i, 0)),
                pl.BlockSpec((D_ROPE, tm), lambda bi, i: (0, i)),
                pl.BlockSpec((D_ROPE, tm), lambda bi, i: (0, i))]
    kq = N_HEADS * D_QK_PAD
    row_major = lambda w: (tok(w), jax.ShapeDtypeStruct((b, s, w), BF16))
    col_major = lambda w: (tok_t(w), jax.ShapeDtypeStruct((b, nt, w, tm), BF16))
    outs = [row_major(kq), col_major(MLA_W)]
    if latent:
        outs = [col_major(kq)] + outs + [row_major(MLA_W), row_major(MLA_W)]
    return pl.pallas_call(
        functools.partial(_inproj_kernel, latent=latent),
        grid=(b, nt),
        in_specs=in_specs,
        out_specs=[o[0] for o in outs],
        out_shape=[o[1] for o in outs],
        compiler_params=pltpu.CompilerParams(dimension_semantics=("parallel", "arbitrary"),
                                             vmem_limit_bytes=VMEM_LIMIT),
        name="inproj_latent" if latent else "inproj_ctx",
    )(x, mods, g, win, wqt, wk, wvt, qg, kvg, ca, sa, ct, st)


def _attn_kernel(qt_ref, kc_ref, vct_ref, kx_ref, vxt_ref, o_ref, s_ref):
    qt = jnp.concatenate([qt_ref[0, j] for j in range(qt_ref.shape[1])], axis=1)
    tq = qt.shape[1]
    nb = vxt_ref.shape[1]
    tk = kx_ref.shape[1] // nb
    lc = kc_ref.shape[1]

    def scores(k):
        s = _dot(k, qt)
        return s, jnp.max(s, axis=0, keepdims=True)

    def keys(j):
        return kx_ref[0, pl.ds(pl.multiple_of(j * tk, tk), tk), :]

    def absorb(s, mx, vt, carry):
        m, l, acc = carry
        m_new = jnp.maximum(m, mx)
        alpha = jnp.exp2(m - m_new)
        p = jnp.exp2(s - m_new)
        l = alpha * l + jnp.sum(p, axis=0, keepdims=True)
        acc = alpha * acc + _dot(vt, p.astype(BF16))
        return m_new, l, acc

    def pair(cur_rows, cur_vt, j, carry, mx):
        s_n, mx_n = scores(keys(j))
        s_ref[1] = s_n
        carry = absorb(s_ref[0, :cur_rows, :], mx, cur_vt, carry)
        s_a, mx_a = scores(keys(j + 1))
        s_ref[0] = s_a
        carry = absorb(s_ref[1], mx_n, vxt_ref[0, j], carry)
        return carry, mx_a

    init = (jnp.full((1, tq), -jnp.inf, F32), jnp.zeros((1, tq), F32), jnp.zeros((D_V, tq), F32))
    s, mx = scores(kc_ref[0])
    s_ref[0, :lc, :] = s
    state = pair(lc, vct_ref[0, 0], 0, init, mx)

    def body(i, state):
        return pair(tk, vxt_ref[0, 2 * i - 1], 2 * i, *state)

    carry, mx = lax.fori_loop(1, nb // 2, body, state)
    _, l, acc = absorb(s_ref[0], mx, vxt_ref[0, nb - 1], carry)
    o_ref[0] = (acc / l).T.astype(o_ref.dtype)


def _attention(qt, kc, vct, kx, vxt):
    b, nt, _, tk = qt.shape
    s = kx.shape[1]
    lc = kc.shape[1]
    nq = min(ATTN_Q_TILES, nt)
    tq = nq * tk
    assert nt % 2 == 0 and nt % nq == 0 and lc <= tk, \
        "latent chunks are absorbed in pairs after one context chunk"
    return pl.pallas_call(
        _attn_kernel,
        grid=(b, N_HEADS, nt // nq),
        in_specs=[pl.BlockSpec((1, nq, D_QK_PAD, tk), lambda bi, h, i: (bi, i, h, 0)),
                  pl.BlockSpec((1, lc, D_QK_PAD), lambda bi, h, i: (bi, 0, h)),
                  pl.BlockSpec((1, 1, D_V, lc), lambda bi, h, i: (bi, 0, h, 0)),
                  pl.BlockSpec((1, s, D_QK_PAD), lambda bi, h, i: (bi, 0, h)),
                  pl.BlockSpec((1, nt, D_V, tk), lambda bi, h, i: (bi, 0, h, 0))],
        out_specs=pl.BlockSpec((1, tq, D_V), lambda bi, h, i: (bi, i, h)),
        out_shape=jax.ShapeDtypeStruct((b, s, MLA_W), BF16),
        scratch_shapes=[pltpu.VMEM((2, tk, tq), F32)],
        compiler_params=pltpu.CompilerParams(
            dimension_semantics=("parallel", "parallel", "arbitrary"), vmem_limit_bytes=VMEM_LIMIT),
        name="attn",
    )(qt, kc, vct, kx, vxt)


def _route(logits_t, bias):
    e, tm = logits_t.shape
    neg = -jnp.inf
    s = 1.0 / (1.0 + jnp.exp(-logits_t))
    sel = s + bias
    io_g = lax.broadcasted_iota(jnp.int32, (GROUP_SIZE, tm), 0)
    gs = []
    for g in range(N_GROUPS):
        blk = sel[g * GROUP_SIZE:(g + 1) * GROUP_SIZE]
        m1 = jnp.max(blk, axis=0, keepdims=True)
        i1 = jnp.min(jnp.where(blk == m1, io_g, GROUP_SIZE), axis=0, keepdims=True)
        m2 = jnp.max(jnp.where(io_g == i1, neg, blk), axis=0, keepdims=True)
        gs.append(m1 + m2)
    masks = []
    for g in range(N_GROUPS):
        rank = jnp.zeros((1, tm), jnp.int32)
        for o in range(N_GROUPS):
            if o == g:
                continue
            ahead = (gs[o] > gs[g]) | (gs[o] == gs[g]) if o < g else (gs[o] > gs[g])
            rank = rank + ahead.astype(jnp.int32)
        masks.append(jnp.broadcast_to(rank < TOPK_GROUPS, (GROUP_SIZE, tm)))
    emask = jnp.concatenate(masks, axis=0)
    cur = jnp.where(emask, sel, neg)
    io_e = lax.broadcasted_iota(jnp.int32, (e, tm), 0)
    chosen = jnp.zeros((e, tm), jnp.bool_)
    slots = []
    for _ in range(TOP_K):
        m = jnp.max(cur, axis=0, keepdims=True)
        i = jnp.min(jnp.where(cur == m, io_e, e), axis=0, keepdims=True)
        hit = io_e == i
        chosen = chosen | hit
        cur = jnp.where(hit, neg, cur)
        slots.append((hit, i))
    wsum = jnp.sum(jnp.where(chosen, s, 0.0), axis=0, keepdims=True)
    return chosen, jnp.where(chosen, s / wsum * ROUTED_SCALE, 0.0), slots


def _post_kernel(om_ref, gb_ref, u_ref, up_ref, un_ref, x_ref, mods_ref, cw_ref, g1_ref, g2_ref,
                 wout_ref, gf_ref, wr_ref, rb_ref, tri_ref,
                 h_ref, xn_ref, ri_ref, rw_ref, cnt_ref, run_ref):
    first = (pl.program_id(0) == 0) & (pl.program_id(1) == 0)

    @pl.when(first)
    def _():
        run_ref[...] = jnp.zeros_like(run_ref)

    i = pl.program_id(1)
    nt = pl.num_programs(1)
    tm = u_ref.shape[1]
    u = u_ref[0].astype(F32)
    prev = jnp.where(i > 0, up_ref[0].astype(F32)[BF16_SUBLANES - 1:BF16_SUBLANES], 0.0)
    nxt = jnp.where(i < nt - 1, un_ref[0].astype(F32)[0:1], 0.0)
    row = lax.broadcasted_iota(jnp.int32, u.shape, 0)
    u_m1 = jnp.where(row == 0, prev, pltpu.roll(u, 1, axis=0))
    u_p1 = jnp.where(row == tm - 1, nxt, pltpu.roll(u, tm - 1, axis=0))
    cw = cw_ref[...]
    y = gb_ref[0].astype(F32) * (cw[0:1] * u_m1 + cw[1:2] * u + cw[2:3] * u_p1)
    o = jnp.concatenate([_rms(om_ref[0].astype(F32), g1_ref[...]).astype(BF16),
                         _rms(y, g2_ref[...]).astype(BF16)], axis=-1)
    ga = mods_ref[0, 2:3, :]
    sf = mods_ref[0, 3:4, :]
    scf = mods_ref[0, 4:5, :]
    h = x_ref[0] + ga * _dot(o, wout_ref[...])
    h_ref[0] = h
    xn = _rms(h, gf_ref[...]) * (1.0 + scf) + sf
    xh, xl = _split_bf16(xn)
    xn_ref[...] = xn
    hh_hl = _dot(xh, wr_ref[...])
    logits = hh_hl[:, :LANES] + hh_hl[:, LANES:] + _dot(xl, wr_ref[:, :LANES])
    chosen, comb_t, slots = _route(logits.T[:N_EXPERTS], rb_ref[...])
    sel = jnp.where(chosen, 1.0, 0.0)
    before = _dot(sel.astype(BF16), tri_ref[...]) + run_ref[...]
    run_ref[...] += jnp.sum(sel, axis=1, keepdims=True)
    cnt_ref[...] = jnp.broadcast_to(run_ref[...], cnt_ref.shape).astype(jnp.int32)
    pick = lambda hit, v: jnp.sum(jnp.where(hit, v, 0.0), axis=0, keepdims=True)
    ri_ref[...] = jnp.concatenate(
        [idx for _, idx in slots] + [pick(hit, before).astype(jnp.int32) for hit, _ in slots], axis=0)
    rw_ref[...] = jnp.concatenate(
        [pick(hit, comb_t) for hit, _ in slots] + [jnp.zeros((LANES - TOP_K, tm), F32)], axis=0).T


def _post(o_mla, gb, u, x, mods, conv_w, g1, g2, wout, gf, wr, rb, *, tm):
    b, s, d = x.shape
    nt = s // tm
    hb = tm // BF16_SUBLANES
    last = s // BF16_SUBLANES - 1
    tri = jnp.asarray(np.triu(np.ones((tm, tm), np.float32), k=1), BF16)
    const = lambda a: pl.BlockSpec(a.shape, lambda bi, i: (0,) * a.ndim, pipeline_mode=pl.Buffered(1))
    tok = lambda w: pl.BlockSpec((1, tm, w), lambda bi, i: (bi, i, 0))
    flat = lambda w: pl.BlockSpec((tm, w), lambda bi, i: (bi * nt + i, 0))
    in_specs = [tok(MLA_W), tok(MLA_W), tok(MLA_W),
                pl.BlockSpec((1, BF16_SUBLANES, MLA_W), lambda bi, i: (bi, jnp.maximum(i * hb - 1, 0), 0)),
                pl.BlockSpec((1, BF16_SUBLANES, MLA_W), lambda bi, i: (bi, jnp.minimum((i + 1) * hb, last), 0)),
                tok(d),
                pl.BlockSpec((1, N_MOD, d), lambda bi, i: (bi, 0, 0)),
                const(conv_w), const(g1), const(g2), const(wout), const(gf), const(wr),
                const(rb), const(tri)]
    t = b * s
    return pl.pallas_call(
        _post_kernel,
        grid=(b, nt),
        in_specs=in_specs,
        out_specs=[tok(d), flat(d),
                   pl.BlockSpec((2 * TOP_K, tm), lambda bi, i: (0, bi * nt + i)),
                   flat(LANES),
                   pl.BlockSpec((N_EXPERTS, LANES), lambda bi, i: (0, 0))],
        out_shape=[jax.ShapeDtypeStruct((b, s, d), F32),
                   jax.ShapeDtypeStruct((t, d), F32),
                   jax.ShapeDtypeStruct((2 * TOP_K, t), jnp.int32),
                   jax.ShapeDtypeStruct((t, LANES), F32),
                   jax.ShapeDtypeStruct((N_EXPERTS, LANES), jnp.int32)],
        scratch_shapes=[pltpu.VMEM((N_EXPERTS, 1), F32)],
        compiler_params=pltpu.CompilerParams(dimension_semantics=("arbitrary", "arbitrary"),
                                             vmem_limit_bytes=VMEM_LIMIT),
        name="post",
    )(o_mla, gb, u, u, u, x, mods, conv_w, g1, g2, wout, gf, wr, rb, tri)


def _positions_kernel(base_ref, ri_ref, pos_ref):
    e = ri_ref[0:TOP_K, :]
    pos = ri_ref[TOP_K:2 * TOP_K, :]
    for j in range(N_EXPERTS):
        pos = pos + jnp.where(e == j, base_ref[j], 0)
    pos_ref[...] = pos


def _positions(base, ri, *, tm):
    t = ri.shape[1]
    return pl.pallas_call(
        _positions_kernel,
        grid_spec=pltpu.PrefetchScalarGridSpec(
            num_scalar_prefetch=1,
            grid=(t // tm,),
            in_specs=[pl.BlockSpec((2 * TOP_K, tm), lambda i, base: (0, i))],
            out_specs=pl.BlockSpec((TOP_K, tm), lambda i, base: (0, i))),
        out_shape=jax.ShapeDtypeStruct((TOP_K, t), jnp.int32),
        compiler_params=pltpu.CompilerParams(dimension_semantics=("arbitrary",)),
        name="positions",
    )(base, ri)


def _dispatch_kernel(zt_ref, pos_ref, xn_ref, xs_ref, zero_ref, sem):
    tm = xn_ref.shape[0]
    tz = zero_ref.shape[0]

    @pl.when(pl.program_id(0) == 0)
    def _():
        zero_ref[...] = jnp.zeros_like(zero_ref)
        n_fill = zt_ref.shape[0]
        fill = lambda j: pltpu.make_async_copy(
            zero_ref, xs_ref.at[pl.ds(jnp.maximum(zt_ref[j], 0) * tz, tz)], sem)
        for j in range(n_fill):
            pl.when(zt_ref[j] >= 0)(fill(j).start)
        for j in range(n_fill):
            pl.when(zt_ref[j] >= 0)(fill(j).wait)

    for t in range(tm):
        for k in range(TOP_K):
            pltpu.make_async_copy(xn_ref.at[pl.ds(t, 1)], xs_ref.at[pl.ds(pos_ref[k, t], 1)],
                                  sem).start(priority=k % 2)
    for _ in range(TOP_K):
        pltpu.make_async_copy(xn_ref, xs_ref.at[pl.ds(0, tm)], sem).wait()


def _dispatch(zt, pos, xn, *, n_rows, tm, tile):
    t, w = xn.shape
    return pl.pallas_call(
        _dispatch_kernel,
        grid_spec=pltpu.PrefetchScalarGridSpec(
            num_scalar_prefetch=1,
            grid=(t // tm,),
            in_specs=[pl.BlockSpec((TOP_K, tm), lambda i, zt: (0, i), memory_space=pltpu.SMEM),
                      pl.BlockSpec((tm, w), lambda i, zt: (i, 0))],
            out_specs=pl.BlockSpec(memory_space=pl.ANY),
            scratch_shapes=[pltpu.VMEM((tile, w), xn.dtype), pltpu.SemaphoreType.DMA(())]),
        out_shape=jax.ShapeDtypeStruct((n_rows, w), xn.dtype),
        compiler_params=pltpu.CompilerParams(dimension_semantics=("arbitrary",),
                                             vmem_limit_bytes=VMEM_LIMIT),
        name="dispatch",
    )(zt, pos, xn)


def _expert_kernel(te_ref, tv_ref, xs_ref, wg_ref, wu_ref, wd_ref, ys_ref):
    nv = tv_ref[pl.program_id(0)]

    @pl.when(nv > 0)
    def _():
        rows = lax.broadcasted_iota(jnp.int32, xs_ref.shape, 0)
        x = jnp.where(rows < nv, xs_ref[...], 0.0).astype(BF16)
        a = (_silu(_dot(x, wg_ref[0])) * _dot(x, wu_ref[0])).astype(BF16)
        ys_ref[...] = _dot(a, wd_ref[0])

    @pl.when(nv == 0)
    def _():
        ys_ref[...] = jnp.zeros_like(ys_ref)


def _experts(te, tv, xs, wg, wu, wd, *, tm):
    n_rows, w = xs.shape
    d = wd.shape[2]
    per_expert = lambda a: pl.BlockSpec((1,) + a.shape[1:], lambda i, te, tv: (te[i], 0, 0))
    return pl.pallas_call(
        _expert_kernel,
        grid_spec=pltpu.PrefetchScalarGridSpec(
            num_scalar_prefetch=2,
            grid=(n_rows // tm,),
            in_specs=[pl.BlockSpec((tm, w), lambda i, te, tv: (i, 0)),
                      per_expert(wg), per_expert(wu), per_expert(wd)],
            out_specs=pl.BlockSpec((tm, d), lambda i, te, tv: (i, 0))),
        out_shape=jax.ShapeDtypeStruct((n_rows, d), F32),
        compiler_params=pltpu.CompilerParams(dimension_semantics=("arbitrary",),
                                             vmem_limit_bytes=VMEM_LIMIT),
        name="experts",
    )(te, tv, xs, wg, wu, wd)


def _combine_kernel(pos_ref, pos_next_ref, rw_ref, xn_ref, h_ref, mods_ref, wsgu_ref, wsd_ref, gfin_ref,
                    ys_ref, o_ref, buf_ref, sems):
    tm, d = xn_ref.shape
    g = pl.program_id(0) * pl.num_programs(1) + pl.program_id(1)
    n = pl.num_programs(0) * pl.num_programs(1)

    def start_rows(p_ref, t, to):
        for k in range(TOP_K):
            pltpu.make_async_copy(ys_ref.at[pl.ds(p_ref[k, t], 1)],
                                  buf_ref.at[to, k, pl.ds(t, 1)], sems.at[to]).start(priority=k % 2)

    def wait_rows(to):
        for k in range(TOP_K):
            pltpu.make_async_copy(ys_ref.at[pl.ds(0, tm)], buf_ref.at[to, k], sems.at[to]).wait()

    @pl.when(g == 0)
    def _():
        def issue(t, carry):
            start_rows(pos_ref, t, 0)
            return carry
        lax.fori_loop(0, tm, issue, 0, unroll=DMA_UNROLL)

    def tile(slot):
        other = 1 - slot
        xb = xn_ref[...].astype(BF16)
        kc, tc = d // COMBINE_SLICES, tm // COMBINE_SLICES
        sgu = None
        for j in range(COMBINE_SLICES):
            part = _dot(xb[:, j * kc:(j + 1) * kc], wsgu_ref[j * kc:(j + 1) * kc, :])
            sgu = part if sgu is None else sgu + part
            for t in range(j * tc, (j + 1) * tc):
                start_rows(pos_next_ref, t, other)
        sf = sgu.shape[1] // 2
        y = _dot((_silu(sgu[:, :sf]) * sgu[:, sf:]).astype(BF16), wsd_ref[...])

        wait_rows(slot)
        rw = rw_ref[...]
        for k in range(TOP_K):
            y = y + rw[:, k:k + 1] * buf_ref[slot, k]
        gate = mods_ref[0, 5:6, :]
        o_ref[0] = _rms(h_ref[0] + gate * y, gfin_ref[...])

        @pl.when(g == n - 1)
        def _():
            wait_rows(other)

    pl.when(g % 2 == 0)(functools.partial(tile, 0))
    pl.when(g % 2 == 1)(functools.partial(tile, 1))


def _combine(pos, rw, xn, h, mods, wsgu, wsd, gfin, ys, *, tm):
    b, s, d = h.shape
    nt = s // tm
    last = b * nt - 1
    const = lambda a: pl.BlockSpec(a.shape, lambda bi, i: (0,) * a.ndim, pipeline_mode=pl.Buffered(1))
    flat = lambda w: pl.BlockSpec((tm, w), lambda bi, i: (bi * nt + i, 0))
    return pl.pallas_call(
        _combine_kernel,
        grid=(b, nt),
        in_specs=[pl.BlockSpec((TOP_K, tm), lambda bi, i: (0, bi * nt + i), memory_space=pltpu.SMEM),
                  pl.BlockSpec((TOP_K, tm), lambda bi, i: (0, jnp.minimum(bi * nt + i + 1, last)),
                               memory_space=pltpu.SMEM),
                  flat(LANES), flat(xn.shape[1]),
                  pl.BlockSpec((1, tm, d), lambda bi, i: (bi, i, 0)),
                  pl.BlockSpec((1, N_MOD, d), lambda bi, i: (bi, 0, 0)),
                  const(wsgu), const(wsd), const(gfin),
                  pl.BlockSpec(memory_space=pl.ANY)],
        out_specs=pl.BlockSpec((1, tm, d), lambda bi, i: (bi, i, 0)),
        out_shape=jax.ShapeDtypeStruct((b, s, d), F32),
        scratch_shapes=[pltpu.VMEM((2, TOP_K, tm, d), F32), pltpu.SemaphoreType.DMA((2,))],
        compiler_params=pltpu.CompilerParams(dimension_semantics=("arbitrary", "arbitrary"),
                                             vmem_limit_bytes=VMEM_LIMIT),
        name="combine",
    )(pos, pos, rw, xn, h, mods, wsgu, wsd, gfin, ys)


def _rope_tables(s):
    nf = D_ROPE // 4
    pos = np.arange(s)
    inv = ROPE_BASE ** (-np.arange(nf, dtype=np.float64) / nf)
    ar = (pos // GRID_W)[:, None] * inv
    ac = (pos % GRID_W)[:, None] * inv
    c64 = np.concatenate([np.cos(ar), np.cos(ar), np.cos(ac), np.cos(ac)], axis=-1)
    s64 = np.concatenate([-np.sin(ar), np.sin(ar), -np.sin(ac), np.sin(ac)], axis=-1)
    return c64.astype(np.float32), s64.astype(np.float32)


def _swap_pairs(w):
    nf = D_ROPE // 4
    return jnp.concatenate([w[..., nf:2 * nf], w[..., :nf], w[..., 3 * nf:], w[..., 2 * nf:3 * nf]], axis=-1)


def kernel(x, c, ctx, c_ctx, w_mod, b_mod, attn_norm, w_in, q_a_norm, w_q_b, kv_a_norm, w_kv_b, conv_w,
           o_norm_mla, o_norm_conv, w_out, ffn_norm, w_router, router_bias, w_exp_gate, w_exp_up,
           w_exp_down, w_sh_gate, w_sh_up, w_sh_down, final_norm):
    b, s, d = x.shape
    lc = ctx.shape[1]
    assert w_mod.shape[0] == 1, "single trunk layer"
    assert s % GRID_W == 0
    assert lc <= TM_IN, "context keys are processed as a single attention chunk"
    tm_in, tm_post = min(TM_IN, s), min(TM_POST, s)

    wi = w_in[0]
    kr_cols = wi[:, C_KR:C_KR + D_ROPE]
    win = jnp.concatenate([wi[:, :C_KR], kr_cols, _swap_pairs(kr_cols), wi[:, C_KR + D_ROPE:]],
                          axis=-1).astype(BF16)
    wq3 = w_q_b[0].reshape(Q_RANK, N_HEADS, D_NOPE + D_ROPE)
    wq = jnp.concatenate([wq3, _swap_pairs(wq3[..., D_NOPE:])], axis=-1)
    wqt = wq.reshape(Q_RANK, N_HEADS * D_QK_PAD).T.astype(BF16)
    wkv3 = w_kv_b[0].reshape(KV_RANK, N_HEADS, D_NOPE + D_V)
    wk = wkv3[..., :D_NOPE].reshape(KV_RANK, MLA_W).astype(BF16)
    wvt = wkv3[..., D_NOPE:].reshape(KV_RANK, MLA_W).T.astype(BF16)
    wr = jnp.pad(w_router[0], ((0, 0), (0, LANES - N_EXPERTS)))
    wr = jnp.concatenate(_split_bf16(wr), axis=-1)
    wg, wu = w_exp_gate[0].astype(BF16), w_exp_up[0].astype(BF16)
    wd = w_exp_down[0].astype(BF16)
    wsgu = jnp.concatenate([w_sh_gate[0], w_sh_up[0]], axis=-1).astype(BF16)
    wsd = w_sh_down[0].astype(BF16)
    row = lambda v: v.reshape(1, -1)

    n_rows = -(-(b + 1) // 8) * 8
    cc = jnp.zeros((n_rows, d), F32).at[:b].set(c).at[b].set(c_ctx)
    mods = _mods(cc, w_mod[0], row(b_mod[0])).reshape(n_rows, N_MOD, d)

    c64, s64 = _rope_tables(s)
    pad = lambda t: np.pad(t, ((0, 0), (0, LANES - D_ROPE)))
    ones_c, zeros_c = np.ones((lc, D_ROPE), np.float32), np.zeros((lc, D_ROPE), np.float32)
    common = (row(attn_norm[0]), win, wqt, wk, wvt, row(q_a_norm[0]), row(kv_a_norm[0]))
    kc, vct = _inproj(ctx, mods, lambda bi: b, *common, pad(ones_c), pad(zeros_c), ones_c.T, zeros_c.T,
                      latent=False, tm=min(TM_IN, lc))
    qt, kx, vxt, gb, u = _inproj(x, mods, lambda bi: bi, *common, pad(c64), pad(s64), c64.T, s64.T,
                                 latent=True, tm=tm_in)

    o_mla = _attention(qt, kc, vct, kx, vxt)

    h, xn, ri, rw, cnt = _post(o_mla, gb, u, x, mods, conv_w[0], row(o_norm_mla[0]), row(o_norm_conv[0]),
                               w_out[0].astype(BF16), row(ffn_norm[0]), wr,
                               router_bias[0].reshape(-1, 1), tm=tm_post)

    tmx = TM_EXPERT
    n_tiles = pl.cdiv(b * s * TOP_K, tmx) + N_EXPERTS
    counts = cnt[:, 0]
    tiles_per = (counts + tmx - 1) // tmx
    tile_end = jnp.cumsum(tiles_per)
    tile_start = tile_end - tiles_per
    base = (tile_start * tmx).astype(jnp.int32)
    tile_ids = jnp.arange(n_tiles, dtype=jnp.int32)
    te = jnp.minimum(jnp.sum(tile_end[None, :] <= tile_ids[:, None], axis=1), N_EXPERTS - 1)
    own = te[:, None] == jnp.arange(N_EXPERTS)[None, :]
    left = jnp.sum(jnp.where(own, counts[None, :] - (tile_ids[:, None] - tile_start[None, :]) * tmx, 0),
                   axis=1)
    tv = jnp.where(tile_ids < tile_end[-1], jnp.clip(left, 0, tmx), 0).astype(jnp.int32)
    tail = tile_end[-1] + jnp.arange(N_EXPERTS)
    zt = jnp.concatenate([jnp.where(tiles_per > 0, tile_end - 1, -1),
                          jnp.where(tail < n_tiles, tail, -1)]).astype(jnp.int32)

    pos = _positions(base, ri, tm=min(TM_POSITIONS, b * s))
    xs = _dispatch(zt, pos, xn, n_rows=n_tiles * tmx, tm=min(TM_DISPATCH, s), tile=tmx)
    ys = _experts(te.astype(jnp.int32), tv, xs, wg, wu, wd, tm=tmx)
    return _combine(pos, rw, xn, h, mods, wsgu, wsd, row(final_norm), ys, tm=min(TM_COMBINE, s))
```

```python
import functools
import math

import jax
import jax.numpy as jnp
import numpy as np
from jax import lax
from jax.experimental import pallas as pl
from jax.experimental.pallas import tpu as pltpu

F32 = jnp.float32
BF16 = jnp.bfloat16

N_HEADS = 8
D_NOPE = 128
D_ROPE = 64
D_V = 128
D_QK_PAD = 256
Q_RANK = 512
KV_RANK = 256
MLA_W = N_HEADS * D_V
GRID_W = 64
ROPE_BASE = 10000.0
ATTN_SCALE = (D_NOPE + D_ROPE) ** -0.5
LOG2E = math.log2(math.e)
N_EXPERTS = 32
N_GROUPS = 4
GROUP_SIZE = N_EXPERTS // N_GROUPS
TOPK_GROUPS = 2
TOP_K = 4
ROUTED_SCALE = 2.5
N_MOD = 6
EPS = 1e-6

C_Q = 0
C_KV = Q_RANK
C_KR = C_KV + KV_RANK
C_GB = C_KR + 128

LANES = 128
BF16_SUBLANES = 16
VMEM_LIMIT = 56 * 1024 * 1024

TM_IN = 512
ATTN_Q_TILES = 8
TM_POST = 512
TM_POSITIONS = 4096
TM_DISPATCH = 256
TM_EXPERT = 512
TM_COMBINE = 256
TN_MODS = 1024
DMA_UNROLL = 8
COMBINE_SLICES = 8


def _rms(x, g):
    return x * lax.rsqrt(jnp.mean(x * x, axis=-1, keepdims=True) + EPS) * g


def _silu(x):
    return x / (1.0 + jnp.exp(-x))


def _split_bf16(x):
    hi = x.astype(BF16)
    lo = (x - hi.astype(F32)).astype(BF16)
    return hi, lo


def _dot(a, b):
    return jnp.dot(a, b, preferred_element_type=F32)


def _mods_kernel(a_ref, w_ref, b_ref, o_ref):
    a = _silu(a_ref[...])
    ah, al = _split_bf16(a)
    wh, wl = _split_bf16(w_ref[...])
    o_ref[...] = _dot(ah, wh) + _dot(al, wh) + _dot(ah, wl) + b_ref[...]


def _mods(cc, w_mod, b_mod):
    rows, d = cc.shape
    n = w_mod.shape[1]
    return pl.pallas_call(
        _mods_kernel,
        grid=(n // TN_MODS,),
        in_specs=[pl.BlockSpec((rows, d), lambda j: (0, 0)),
                  pl.BlockSpec((d, TN_MODS), lambda j: (0, j)),
                  pl.BlockSpec((1, TN_MODS), lambda j: (0, j))],
        out_specs=pl.BlockSpec((rows, TN_MODS), lambda j: (0, j)),
        out_shape=jax.ShapeDtypeStruct((rows, n), F32),
        compiler_params=pltpu.CompilerParams(dimension_semantics=("arbitrary",),
                                             vmem_limit_bytes=VMEM_LIMIT),
        name="mods",
    )(cc, w_mod, b_mod)


def _inproj_kernel(x_ref, mods_ref, g_ref, win_ref, wcv_ref, wqt_ref, wk_ref, wvt_ref, qg_ref, kvg_ref,
                   ca_ref, sa_ref, ct_ref, st_ref, *out_refs, latent):
    x = x_ref[0]
    shift = mods_ref[0, 0:1, :]
    scale = mods_ref[0, 1:2, :]
    nx = (_rms(x, g_ref[...]) * (1.0 + scale) + shift).astype(BF16)

    if latent:
        qt_ref, k_ref, vt_ref, gb_ref, u_ref = out_refs
        q_a = _dot(nx, win_ref[:, C_Q:C_KV])
        qnt = _rms(q_a, qg_ref[...]).T.astype(BF16)
        qt = _dot(wqt_ref[...], qnt) * (ATTN_SCALE * LOG2E)
        ct = ct_ref[...]
        st = st_ref[...]
        for h in range(N_HEADS):
            r0 = h * D_QK_PAD
            r1 = r0 + D_NOPE
            r2 = r1 + D_ROPE
            qt_ref[0, 0, r0:r1, :] = qt[r0:r1].astype(BF16)
            qt_ref[0, 0, r1:r2, :] = (qt[r1:r2] * ct + qt[r2:r0 + D_QK_PAD] * st).astype(BF16)
            qt_ref[0, 0, r2:r0 + D_QK_PAD, :] = jnp.zeros((D_ROPE, qt.shape[1]), BF16)
        gb_ref[0] = _dot(nx, wcv_ref[:, :MLA_W]).astype(BF16)
        g_c = _dot(nx, wcv_ref[:, MLA_W:2 * MLA_W])
        hh = _dot(nx, wcv_ref[:, 2 * MLA_W:])
        u_ref[0] = (g_c * hh).astype(BF16)
    else:
        k_ref, vt_ref = out_refs

    kv_a = _dot(nx, win_ref[:, C_KV:C_KR])
    kvn = _rms(kv_a, kvg_ref[...])
    kn = _dot(kvn.astype(BF16), wk_ref[...])
    vt_ref[0, 0] = _dot(wvt_ref[...], kvn.T.astype(BF16)).astype(BF16)
    blk = _dot(nx, win_ref[:, C_KR:C_GB])
    kr = (blk * ca_ref[...] + pltpu.roll(blk, D_ROPE, axis=1) * sa_ref[...]).astype(BF16)
    for h in range(N_HEADS):
        c0 = h * D_QK_PAD
        k_ref[0, :, c0:c0 + D_NOPE] = kn[:, h * D_NOPE:(h + 1) * D_NOPE].astype(BF16)
        k_ref[0, :, c0 + D_NOPE:c0 + D_QK_PAD] = kr


def _inproj(x, mods, mod_row, g, win, wcv, wqt, wk, wvt, qg, kvg, ca, sa, ct, st, *, latent, tm):
    b, s, d = x.shape
    nt = s // tm
    const = lambda a: pl.BlockSpec(a.shape, lambda bi, i: (0,) * a.ndim, pipeline_mode=pl.Buffered(1))
    tok = lambda w: pl.BlockSpec((1, tm, w), lambda bi, i: (bi, i, 0))
    tok_t = lambda w: pl.BlockSpec((1, 1, w, tm), lambda bi, i: (bi, i, 0, 0))
    in_specs = [tok(d),
                pl.BlockSpec((1, N_MOD, d), lambda bi, i: (mod_row(bi), 0, 0)),
                const(g), const(win), const(wcv), const(wqt), const(wk), const(wvt), const(qg), const(kvg),
                pl.BlockSpec((tm, LANES), lambda bi, i: (i, 0)),
                pl.BlockSpec((tm, LANES), lambda bi, i: (i, 0)),
                pl.BlockSpec((D_ROPE, tm), lambda bi, i: (0, i)),
                pl.BlockSpec((D_ROPE, tm), lambda bi, i: (0, i))]
    kq = N_HEADS * D_QK_PAD
    row_major = lambda w: (tok(w), jax.ShapeDtypeStruct((b, s, w), BF16))
    col_major = lambda w: (tok_t(w), jax.ShapeDtypeStruct((b, nt, w, tm), BF16))
    outs = [row_major(kq), col_major(MLA_W)]
    if latent:
        outs = [col_major(kq)] + outs + [row_major(MLA_W), row_major(MLA_W)]
    return pl.pallas_call(
        functools.partial(_inproj_kernel, latent=latent),
        grid=(b, nt),
        in_specs=in_specs,
        out_specs=[o[0] for o in outs],
        out_shape=[o[1] for o in outs],
        compiler_params=pltpu.CompilerParams(dimension_semantics=("parallel", "arbitrary"),
                                             vmem_limit_bytes=VMEM_LIMIT),
        name="inproj_latent" if latent else "inproj_ctx",
    )(x, mods, g, win, wcv, wqt, wk, wvt, qg, kvg, ca, sa, ct, st)


def _attn_kernel(qt_ref, kc_ref, vct_ref, kx_ref, vxt_ref, o_ref, s_ref):
    qt = jnp.concatenate([qt_ref[0, j] for j in range(qt_ref.shape[1])], axis=1)
    tq = qt.shape[1]
    nb = vxt_ref.shape[1]
    tk = kx_ref.shape[1] // nb
    lc = kc_ref.shape[1]

    def scores(k):
        s = _dot(k, qt)
        return s, jnp.max(s, axis=0, keepdims=True)

    def keys(j):
        return kx_ref[0, pl.ds(pl.multiple_of(j * tk, tk), tk), :]

    def absorb(s, mx, vt, carry):
        m, l, acc = carry
        m_new = jnp.maximum(m, mx)
        alpha = jnp.exp2(m - m_new)
        p = jnp.exp2(s - m_new)
        l = alpha * l + jnp.sum(p, axis=0, keepdims=True)
        acc = alpha * acc + _dot(vt, p.astype(BF16))
        return m_new, l, acc

    def pair(cur_rows, cur_vt, j, carry, mx):
        s_n, mx_n = scores(keys(j))
        s_ref[1] = s_n
        carry = absorb(s_ref[0, :cur_rows, :], mx, cur_vt, carry)
        s_a, mx_a = scores(keys(j + 1))
        s_ref[0] = s_a
        carry = absorb(s_ref[1], mx_n, vxt_ref[0, j], carry)
        return carry, mx_a

    init = (jnp.full((1, tq), -jnp.inf, F32), jnp.zeros((1, tq), F32), jnp.zeros((D_V, tq), F32))
    s, mx = scores(kc_ref[0])
    s_ref[0, :lc, :] = s
    state = pair(lc, vct_ref[0, 0], 0, init, mx)

    def body(i, state):
        return pair(tk, vxt_ref[0, 2 * i - 1], 2 * i, *state)

    carry, mx = lax.fori_loop(1, nb // 2, body, state)
    _, l, acc = absorb(s_ref[0], mx, vxt_ref[0, nb - 1], carry)
    o_ref[0] = (acc / l).T.astype(o_ref.dtype)


def _attention(qt, kc, vct, kx, vxt):
    b, nt, _, tk = qt.shape
    s = kx.shape[1]
    lc = kc.shape[1]
    nq = min(ATTN_Q_TILES, nt)
    tq = nq * tk
    assert nt % 2 == 0 and nt % nq == 0 and lc <= tk, \
        "latent chunks are absorbed in pairs after one context chunk"
    return pl.pallas_call(
        _attn_kernel,
        grid=(b, N_HEADS, nt // nq),
        in_specs=[pl.BlockSpec((1, nq, D_QK_PAD, tk), lambda bi, h, i: (bi, i, h, 0)),
                  pl.BlockSpec((1, lc, D_QK_PAD), lambda bi, h, i: (bi, 0, h)),
                  pl.BlockSpec((1, 1, D_V, lc), lambda bi, h, i: (bi, 0, h, 0)),
                  pl.BlockSpec((1, s, D_QK_PAD), lambda bi, h, i: (bi, 0, h)),
                  pl.BlockSpec((1, nt, D_V, tk), lambda bi, h, i: (bi, 0, h, 0))],
        out_specs=pl.BlockSpec((1, tq, D_V), lambda bi, h, i: (bi, i, h)),
        out_shape=jax.ShapeDtypeStruct((b, s, MLA_W), BF16),
        scratch_shapes=[pltpu.VMEM((2, tk, tq), F32)],
        compiler_params=pltpu.CompilerParams(
            dimension_semantics=("parallel", "parallel", "arbitrary"), vmem_limit_bytes=VMEM_LIMIT),
        name="attn",
    )(qt, kc, vct, kx, vxt)


def _route(logits_t, bias):
    e, tm = logits_t.shape
    neg = -jnp.inf
    s = 1.0 / (1.0 + jnp.exp(-logits_t))
    sel = s + bias
    io_g = lax.broadcasted_iota(jnp.int32, (GROUP_SIZE, tm), 0)
    gs = []
    for g in range(N_GROUPS):
        blk = sel[g * GROUP_SIZE:(g + 1) * GROUP_SIZE]
        m1 = jnp.max(blk, axis=0, keepdims=True)
        i1 = jnp.min(jnp.where(blk == m1, io_g, GROUP_SIZE), axis=0, keepdims=True)
        m2 = jnp.max(jnp.where(io_g == i1, neg, blk), axis=0, keepdims=True)
        gs.append(m1 + m2)
    masks = []
    for g in range(N_GROUPS):
        rank = jnp.zeros((1, tm), jnp.int32)
        for o in range(N_GROUPS):
            if o == g:
                continue
            ahead = (gs[o] > gs[g]) | (gs[o] == gs[g]) if o < g else (gs[o] > gs[g])
            rank = rank + ahead.astype(jnp.int32)
        masks.append(jnp.broadcast_to(rank < TOPK_GROUPS, (GROUP_SIZE, tm)))
    emask = jnp.concatenate(masks, axis=0)
    cur = jnp.where(emask, sel, neg)
    io_e = lax.broadcasted_iota(jnp.int32, (e, tm), 0)
    chosen = jnp.zeros((e, tm), jnp.bool_)
    slots = []
    for _ in range(TOP_K):
        m = jnp.max(cur, axis=0, keepdims=True)
        i = jnp.min(jnp.where(cur == m, io_e, e), axis=0, keepdims=True)
        hit = io_e == i
        chosen = chosen | hit
        cur = jnp.where(hit, neg, cur)
        slots.append((hit, i))
    wsum = jnp.sum(jnp.where(chosen, s, 0.0), axis=0, keepdims=True)
    return chosen, jnp.where(chosen, s / wsum * ROUTED_SCALE, 0.0), slots


def _post_kernel(om_ref, gb_ref, u_ref, up_ref, un_ref, x_ref, mods_ref, cw_ref, g1_ref, g2_ref,
                 wout_ref, gf_ref, wr_ref, rb_ref, tri_ref,
                 h_ref, xn_ref, ri_ref, rw_ref, cnt_ref, run_ref):
    first = (pl.program_id(0) == 0) & (pl.program_id(1) == 0)

    @pl.when(first)
    def _():
        run_ref[...] = jnp.zeros_like(run_ref)

    i = pl.program_id(1)
    nt = pl.num_programs(1)
    tm = u_ref.shape[1]
    u = u_ref[0].astype(F32)
    prev = jnp.where(i > 0, up_ref[0].astype(F32)[BF16_SUBLANES - 1:BF16_SUBLANES], 0.0)
    nxt = jnp.where(i < nt - 1, un_ref[0].astype(F32)[0:1], 0.0)
    row = lax.broadcasted_iota(jnp.int32, u.shape, 0)
    u_m1 = jnp.where(row == 0, prev, pltpu.roll(u, 1, axis=0))
    u_p1 = jnp.where(row == tm - 1, nxt, pltpu.roll(u, tm - 1, axis=0))
    cw = cw_ref[...]
    y = gb_ref[0].astype(F32) * (cw[0:1] * u_m1 + cw[1:2] * u + cw[2:3] * u_p1)
    o = jnp.concatenate([_rms(om_ref[0].astype(F32), g1_ref[...]).astype(BF16),
                         _rms(y, g2_ref[...]).astype(BF16)], axis=-1)
    ga = mods_ref[0, 2:3, :]
    sf = mods_ref[0, 3:4, :]
    scf = mods_ref[0, 4:5, :]
    h = x_ref[0] + ga * _dot(o, wout_ref[...])
    h_ref[0] = h
    xn = _rms(h, gf_ref[...]) * (1.0 + scf) + sf
    xh, xl = _split_bf16(xn)
    xn_ref[...] = xn
    hh_hl = _dot(xh, wr_ref[...])
    logits = hh_hl[:, :LANES] + hh_hl[:, LANES:] + _dot(xl, wr_ref[:, :LANES])
    chosen, comb_t, slots = _route(logits.T[:N_EXPERTS], rb_ref[...])
    sel = jnp.where(chosen, 1.0, 0.0)
    before = _dot(sel.astype(BF16), tri_ref[...]) + run_ref[...]
    run_ref[...] += jnp.sum(sel, axis=1, keepdims=True)
    cnt_ref[...] = jnp.broadcast_to(run_ref[...], cnt_ref.shape).astype(jnp.int32)
    pick = lambda hit, v: jnp.sum(jnp.where(hit, v, 0.0), axis=0, keepdims=True)
    ri_ref[...] = jnp.concatenate(
        [idx for _, idx in slots] + [pick(hit, before).astype(jnp.int32) for hit, _ in slots], axis=0)
    rw_ref[...] = jnp.concatenate(
        [pick(hit, comb_t) for hit, _ in slots] + [jnp.zeros((LANES - TOP_K, tm), F32)], axis=0).T


def _post(o_mla, gb, u, x, mods, conv_w, g1, g2, wout, gf, wr, rb, *, tm):
    b, s, d = x.shape
    nt = s // tm
    hb = tm // BF16_SUBLANES
    last = s // BF16_SUBLANES - 1
    tri = jnp.asarray(np.triu(np.ones((tm, tm), np.float32), k=1), BF16)
    const = lambda a: pl.BlockSpec(a.shape, lambda bi, i: (0,) * a.ndim, pipeline_mode=pl.Buffered(1))
    tok = lambda w: pl.BlockSpec((1, tm, w), lambda bi, i: (bi, i, 0))
    flat = lambda w: pl.BlockSpec((tm, w), lambda bi, i: (bi * nt + i, 0))
    in_specs = [tok(MLA_W), tok(MLA_W), tok(MLA_W),
                pl.BlockSpec((1, BF16_SUBLANES, MLA_W), lambda bi, i: (bi, jnp.maximum(i * hb - 1, 0), 0)),
                pl.BlockSpec((1, BF16_SUBLANES, MLA_W), lambda bi, i: (bi, jnp.minimum((i + 1) * hb, last), 0)),
                tok(d),
                pl.BlockSpec((1, N_MOD, d), lambda bi, i: (bi, 0, 0)),
                const(conv_w), const(g1), const(g2), const(wout), const(gf), const(wr),
                const(rb), const(tri)]
    t = b * s
    return pl.pallas_call(
        _post_kernel,
        grid=(b, nt),
        in_specs=in_specs,
        out_specs=[tok(d), flat(d),
                   pl.BlockSpec((2 * TOP_K, tm), lambda bi, i: (0, bi * nt + i)),
                   flat(LANES),
                   pl.BlockSpec((N_EXPERTS, LANES), lambda bi, i: (0, 0))],
        out_shape=[jax.ShapeDtypeStruct((b, s, d), F32),
                   jax.ShapeDtypeStruct((t, d), F32),
                   jax.ShapeDtypeStruct((2 * TOP_K, t), jnp.int32),
                   jax.ShapeDtypeStruct((t, LANES), F32),
                   jax.ShapeDtypeStruct((N_EXPERTS, LANES), jnp.int32)],
        scratch_shapes=[pltpu.VMEM((N_EXPERTS, 1), F32)],
        compiler_params=pltpu.CompilerParams(dimension_semantics=("arbitrary", "arbitrary"),
                                             vmem_limit_bytes=VMEM_LIMIT),
        name="post",
    )(o_mla, gb, u, u, u, x, mods, conv_w, g1, g2, wout, gf, wr, rb, tri)


def _positions_kernel(base_ref, ri_ref, pos_ref):
    e = ri_ref[0:TOP_K, :]
    pos = ri_ref[TOP_K:2 * TOP_K, :]
    for j in range(N_EXPERTS):
        pos = pos + jnp.where(e == j, base_ref[j], 0)
    pos_ref[...] = pos


def _positions(base, ri, *, tm):
    t = ri.shape[1]
    return pl.pallas_call(
        _positions_kernel,
        grid_spec=pltpu.PrefetchScalarGridSpec(
            num_scalar_prefetch=1,
            grid=(t // tm,),
            in_specs=[pl.BlockSpec((2 * TOP_K, tm), lambda i, base: (0, i))],
            out_specs=pl.BlockSpec((TOP_K, tm), lambda i, base: (0, i))),
        out_shape=jax.ShapeDtypeStruct((TOP_K, t), jnp.int32),
        compiler_params=pltpu.CompilerParams(dimension_semantics=("arbitrary",)),
        name="positions",
    )(base, ri)


def _dispatch_kernel(zt_ref, pos_ref, xn_ref, xs_ref, zero_ref, sem):
    tm = xn_ref.shape[0]
    tz = zero_ref.shape[0]

    @pl.when(pl.program_id(0) == 0)
    def _():
        zero_ref[...] = jnp.zeros_like(zero_ref)
        n_fill = zt_ref.shape[0]
        fill = lambda j: pltpu.make_async_copy(
            zero_ref, xs_ref.at[pl.ds(jnp.maximum(zt_ref[j], 0) * tz, tz)], sem)
        for j in range(n_fill):
            pl.when(zt_ref[j] >= 0)(fill(j).start)
        for j in range(n_fill):
            pl.when(zt_ref[j] >= 0)(fill(j).wait)

    for t in range(tm):
        for k in range(TOP_K):
            pltpu.make_async_copy(xn_ref.at[pl.ds(t, 1)], xs_ref.at[pl.ds(pos_ref[k, t], 1)],
                                  sem).start(priority=k % 2)
    for _ in range(TOP_K):
        pltpu.make_async_copy(xn_ref, xs_ref.at[pl.ds(0, tm)], sem).wait()


def _dispatch(zt, pos, xn, *, n_rows, tm, tile):
    t, w = xn.shape
    return pl.pallas_call(
        _dispatch_kernel,
        grid_spec=pltpu.PrefetchScalarGridSpec(
            num_scalar_prefetch=1,
            grid=(t // tm,),
            in_specs=[pl.BlockSpec((TOP_K, tm), lambda i, zt: (0, i), memory_space=pltpu.SMEM),
                      pl.BlockSpec((tm, w), lambda i, zt: (i, 0))],
            out_specs=pl.BlockSpec(memory_space=pl.ANY),
            scratch_shapes=[pltpu.VMEM((tile, w), xn.dtype), pltpu.SemaphoreType.DMA(())]),
        out_shape=jax.ShapeDtypeStruct((n_rows, w), xn.dtype),
        compiler_params=pltpu.CompilerParams(dimension_semantics=("arbitrary",),
                                             vmem_limit_bytes=VMEM_LIMIT),
        name="dispatch",
    )(zt, pos, xn)


def _expert_kernel(te_ref, tv_ref, xs_ref, wg_ref, wu_ref, wd_ref, ys_ref):
    nv = tv_ref[pl.program_id(0)]

    @pl.when(nv > 0)
    def _():
        rows = lax.broadcasted_iota(jnp.int32, xs_ref.shape, 0)
        x = jnp.where(rows < nv, xs_ref[...], 0.0).astype(BF16)
        a = (_silu(_dot(x, wg_ref[0])) * _dot(x, wu_ref[0])).astype(BF16)
        ys_ref[...] = _dot(a, wd_ref[0])

    @pl.when(nv == 0)
    def _():
        ys_ref[...] = jnp.zeros_like(ys_ref)


def _experts(te, tv, xs, wg, wu, wd, *, tm):
    n_rows, w = xs.shape
    d = wd.shape[2]
    per_expert = lambda a: pl.BlockSpec((1,) + a.shape[1:], lambda i, te, tv: (te[i], 0, 0))
    return pl.pallas_call(
        _expert_kernel,
        grid_spec=pltpu.PrefetchScalarGridSpec(
            num_scalar_prefetch=2,
            grid=(n_rows // tm,),
            in_specs=[pl.BlockSpec((tm, w), lambda i, te, tv: (i, 0)),
                      per_expert(wg), per_expert(wu), per_expert(wd)],
            out_specs=pl.BlockSpec((tm, d), lambda i, te, tv: (i, 0))),
        out_shape=jax.ShapeDtypeStruct((n_rows, d), F32),
        compiler_params=pltpu.CompilerParams(dimension_semantics=("arbitrary",),
                                             vmem_limit_bytes=VMEM_LIMIT),
        name="experts",
    )(te, tv, xs, wg, wu, wd)


def _combine_kernel(pos_ref, pos_next_ref, rw_ref, xn_ref, h_ref, mods_ref, wsgu_ref, wsd_ref, gfin_ref,
                    ys_ref, o_ref, buf_ref, sems):
    tm, d = xn_ref.shape
    g = pl.program_id(0) * pl.num_programs(1) + pl.program_id(1)
    n = pl.num_programs(0) * pl.num_programs(1)

    def start_rows(p_ref, t, to):
        for k in range(TOP_K):
            pltpu.make_async_copy(ys_ref.at[pl.ds(p_ref[k, t], 1)],
                                  buf_ref.at[to, k, pl.ds(t, 1)], sems.at[to]).start(priority=k % 2)

    def wait_rows(to):
        for k in range(TOP_K):
            pltpu.make_async_copy(ys_ref.at[pl.ds(0, tm)], buf_ref.at[to, k], sems.at[to]).wait()

    @pl.when(g == 0)
    def _():
        def issue(t, carry):
            start_rows(pos_ref, t, 0)
            return carry
        lax.fori_loop(0, tm, issue, 0, unroll=DMA_UNROLL)

    def tile(slot):
        other = 1 - slot
        xb = xn_ref[...].astype(BF16)
        kc, tc = d // COMBINE_SLICES, tm // COMBINE_SLICES
        sgu = None
        for j in range(COMBINE_SLICES):
            part = _dot(xb[:, j * kc:(j + 1) * kc], wsgu_ref[j * kc:(j + 1) * kc, :])
            sgu = part if sgu is None else sgu + part
            for t in range(j * tc, (j + 1) * tc):
                start_rows(pos_next_ref, t, other)
        sf = sgu.shape[1] // 2
        y = _dot((_silu(sgu[:, :sf]) * sgu[:, sf:]).astype(BF16), wsd_ref[...])

        wait_rows(slot)
        rw = rw_ref[...]
        for k in range(TOP_K):
            y = y + rw[:, k:k + 1] * buf_ref[slot, k]
        gate = mods_ref[0, 5:6, :]
        o_ref[0] = _rms(h_ref[0] + gate * y, gfin_ref[...])

        @pl.when(g == n - 1)
        def _():
            wait_rows(other)

    pl.when(g % 2 == 0)(functools.partial(tile, 0))
    pl.when(g % 2 == 1)(functools.partial(tile, 1))


def _combine(pos, rw, xn, h, mods, wsgu, wsd, gfin, ys, *, tm):
    b, s, d = h.shape
    nt = s // tm
    last = b * nt - 1
    const = lambda a: pl.BlockSpec(a.shape, lambda bi, i: (0,) * a.ndim, pipeline_mode=pl.Buffered(1))
    flat = lambda w: pl.BlockSpec((tm, w), lambda bi, i: (bi * nt + i, 0))
    return pl.pallas_call(
        _combine_kernel,
        grid=(b, nt),
        in_specs=[pl.BlockSpec((TOP_K, tm), lambda bi, i: (0, bi * nt + i), memory_space=pltpu.SMEM),
                  pl.BlockSpec((TOP_K, tm), lambda bi, i: (0, jnp.minimum(bi * nt + i + 1, last)),
                               memory_space=pltpu.SMEM),
                  flat(LANES), flat(xn.shape[1]),
                  pl.BlockSpec((1, tm, d), lambda bi, i: (bi, i, 0)),
                  pl.BlockSpec((1, N_MOD, d), lambda bi, i: (bi, 0, 0)),
                  const(wsgu), const(wsd), const(gfin),
                  pl.BlockSpec(memory_space=pl.ANY)],
        out_specs=pl.BlockSpec((1, tm, d), lambda bi, i: (bi, i, 0)),
        out_shape=jax.ShapeDtypeStruct((b, s, d), F32),
        scratch_shapes=[pltpu.VMEM((2, TOP_K, tm, d), F32), pltpu.SemaphoreType.DMA((2,))],
        compiler_params=pltpu.CompilerParams(dimension_semantics=("arbitrary", "arbitrary"),
                                             vmem_limit_bytes=VMEM_LIMIT),
        name="combine",
    )(pos, pos, rw, xn, h, mods, wsgu, wsd, gfin, ys)


def _rope_tables(s):
    nf = D_ROPE // 4
    pos = np.arange(s)
    inv = ROPE_BASE ** (-np.arange(nf, dtype=np.float64) / nf)
    ar = (pos // GRID_W)[:, None] * inv
    ac = (pos % GRID_W)[:, None] * inv
    c64 = np.concatenate([np.cos(ar), np.cos(ar), np.cos(ac), np.cos(ac)], axis=-1)
    s64 = np.concatenate([-np.sin(ar), np.sin(ar), -np.sin(ac), np.sin(ac)], axis=-1)
    return c64.astype(np.float32), s64.astype(np.float32)


def _swap_pairs(w):
    nf = D_ROPE // 4
    return jnp.concatenate([w[..., nf:2 * nf], w[..., :nf], w[..., 3 * nf:], w[..., 2 * nf:3 * nf]], axis=-1)


def kernel(x, c, ctx, c_ctx, w_mod, b_mod, attn_norm, w_in, q_a_norm, w_q_b, kv_a_norm, w_kv_b, conv_w,
           o_norm_mla, o_norm_conv, w_out, ffn_norm, w_router, router_bias, w_exp_gate, w_exp_up,
           w_exp_down, w_sh_gate, w_sh_up, w_sh_down, final_norm):
    b, s, d = x.shape
    lc = ctx.shape[1]
    assert w_mod.shape[0] == 1, "single trunk layer"
    assert s % GRID_W == 0
    assert lc <= TM_IN, "context keys are processed as a single attention chunk"
    tm_in, tm_post = min(TM_IN, s), min(TM_POST, s)

    wi = w_in[0]
    kr_cols = wi[:, C_KR:C_KR + D_ROPE]
    win = jnp.concatenate([wi[:, :C_KR], kr_cols, _swap_pairs(kr_cols)], axis=-1).astype(BF16)
    wcv = wi[:, C_KR + D_ROPE:].astype(BF16)
    wq3 = w_q_b[0].reshape(Q_RANK, N_HEADS, D_NOPE + D_ROPE)
    wq = jnp.concatenate([wq3, _swap_pairs(wq3[..., D_NOPE:])], axis=-1)
    wqt = wq.reshape(Q_RANK, N_HEADS * D_QK_PAD).T.astype(BF16)
    wkv3 = w_kv_b[0].reshape(KV_RANK, N_HEADS, D_NOPE + D_V)
    wk = wkv3[..., :D_NOPE].reshape(KV_RANK, MLA_W).astype(BF16)
    wvt = wkv3[..., D_NOPE:].reshape(KV_RANK, MLA_W).T.astype(BF16)
    wr = jnp.pad(w_router[0], ((0, 0), (0, LANES - N_EXPERTS)))
    wr = jnp.concatenate(_split_bf16(wr), axis=-1)
    wg, wu = w_exp_gate[0].astype(BF16), w_exp_up[0].astype(BF16)
    wd = w_exp_down[0].astype(BF16)
    wsgu = jnp.concatenate([w_sh_gate[0], w_sh_up[0]], axis=-1).astype(BF16)
    wsd = w_sh_down[0].astype(BF16)
    row = lambda v: v.reshape(1, -1)

    n_rows = -(-(b + 1) // 8) * 8
    cc = jnp.zeros((n_rows, d), F32).at[:b].set(c).at[b].set(c_ctx)
    mods = _mods(cc, w_mod[0], row(b_mod[0])).reshape(n_rows, N_MOD, d)

    c64, s64 = _rope_tables(s)
    pad = lambda t: np.pad(t, ((0, 0), (0, LANES - D_ROPE)))
    ones_c, zeros_c = np.ones((lc, D_ROPE), np.float32), np.zeros((lc, D_ROPE), np.float32)
    tail = (wqt, wk, wvt, row(q_a_norm[0]), row(kv_a_norm[0]))
    common = (row(attn_norm[0]), win, wcv) + tail
    ctx_only = (row(attn_norm[0]), win, wcv[:, :LANES]) + tail
    kc, vct = _inproj(ctx, mods, lambda bi: b, *ctx_only, pad(ones_c), pad(zeros_c), ones_c.T, zeros_c.T,
                      latent=False, tm=min(TM_IN, lc))
    qt, kx, vxt, gb, u = _inproj(x, mods, lambda bi: bi, *common, pad(c64), pad(s64), c64.T, s64.T,
                                 latent=True, tm=tm_in)

    o_mla = _attention(qt, kc, vct, kx, vxt)

    h, xn, ri, rw, cnt = _post(o_mla, gb, u, x, mods, conv_w[0], row(o_norm_mla[0]), row(o_norm_conv[0]),
                               w_out[0].astype(BF16), row(ffn_norm[0]), wr,
                               router_bias[0].reshape(-1, 1), tm=tm_post)

    tmx = TM_EXPERT
    n_tiles = pl.cdiv(b * s * TOP_K, tmx) + N_EXPERTS
    counts = cnt[:, 0]
    tiles_per = (counts + tmx - 1) // tmx
    tile_end = jnp.cumsum(tiles_per)
    tile_start = tile_end - tiles_per
    base = (tile_start * tmx).astype(jnp.int32)
    tile_ids = jnp.arange(n_tiles, dtype=jnp.int32)
    te = jnp.minimum(jnp.sum(tile_end[None, :] <= tile_ids[:, None], axis=1), N_EXPERTS - 1)
    own = te[:, None] == jnp.arange(N_EXPERTS)[None, :]
    left = jnp.sum(jnp.where(own, counts[None, :] - (tile_ids[:, None] - tile_start[None, :]) * tmx, 0),
                   axis=1)
    tv = jnp.where(tile_ids < tile_end[-1], jnp.clip(left, 0, tmx), 0).astype(jnp.int32)
    tail = tile_end[-1] + jnp.arange(N_EXPERTS)
    zt = jnp.concatenate([jnp.where(tiles_per > 0, tile_end - 1, -1),
                          jnp.where(tail < n_tiles, tail, -1)]).astype(jnp.int32)

    pos = _positions(base, ri, tm=min(TM_POSITIONS, b * s))
    xs = _dispatch(zt, pos, xn, n_rows=n_tiles * tmx, tm=min(TM_DISPATCH, s), tile=tmx)
    ys = _experts(te.astype(jnp.int32), tv, xs, wg, wu, wd, tm=tmx)
    return _combine(pos, rw, xn, h, mods, wsgu, wsd, row(final_norm), ys, tm=min(TM_COMBINE, s))
```

```python
import functools
import math

import jax
import jax.numpy as jnp
import numpy as np
from jax import lax
from jax.experimental import pallas as pl
from jax.experimental.pallas import tpu as pltpu

F32 = jnp.float32
BF16 = jnp.bfloat16

N_HEADS = 8
D_NOPE = 128
D_ROPE = 64
D_V = 128
D_QK_PAD = 256
Q_RANK = 512
KV_RANK = 256
MLA_W = N_HEADS * D_V
GRID_W = 64
ROPE_BASE = 10000.0
ATTN_SCALE = (D_NOPE + D_ROPE) ** -0.5
LOG2E = math.log2(math.e)
N_EXPERTS = 32
N_GROUPS = 4
GROUP_SIZE = N_EXPERTS // N_GROUPS
TOPK_GROUPS = 2
TOP_K = 4
ROUTED_SCALE = 2.5
N_MOD = 6
EPS = 1e-6

C_Q = 0
C_KV = Q_RANK
C_KR = C_KV + KV_RANK
C_GB = C_KR + 128

LANES = 128
BF16_SUBLANES = 16
VMEM_LIMIT = 56 * 1024 * 1024

TM_IN = 512
ATTN_Q_TILES = 8
TM_POST = 512
TM_POSITIONS = 4096
TM_DISPATCH = 256
TM_EXPERT = 512
TM_COMBINE = 256
TN_MODS = 1024
DMA_UNROLL = 8
COMBINE_SLICES = 8


def _rms(x, g):
    return x * lax.rsqrt(jnp.mean(x * x, axis=-1, keepdims=True) + EPS) * g


def _silu(x):
    return x / (1.0 + jnp.exp(-x))


def _split_bf16(x):
    hi = x.astype(BF16)
    lo = (x - hi.astype(F32)).astype(BF16)
    return hi, lo


def _dot(a, b):
    return jnp.dot(a, b, preferred_element_type=F32)


def _mods_kernel(a_ref, w_ref, b_ref, o_ref):
    a = _silu(a_ref[...])
    ah, al = _split_bf16(a)
    wh, wl = _split_bf16(w_ref[...])
    o_ref[...] = _dot(ah, wh) + _dot(al, wh) + _dot(ah, wl) + b_ref[...]


def _mods(cc, w_mod, b_mod):
    rows, d = cc.shape
    n = w_mod.shape[1]
    return pl.pallas_call(
        _mods_kernel,
        grid=(n // TN_MODS,),
        in_specs=[pl.BlockSpec((rows, d), lambda j: (0, 0)),
                  pl.BlockSpec((d, TN_MODS), lambda j: (0, j)),
                  pl.BlockSpec((1, TN_MODS), lambda j: (0, j))],
        out_specs=pl.BlockSpec((rows, TN_MODS), lambda j: (0, j)),
        out_shape=jax.ShapeDtypeStruct((rows, n), F32),
        compiler_params=pltpu.CompilerParams(dimension_semantics=("arbitrary",),
                                             vmem_limit_bytes=VMEM_LIMIT),
        name="mods",
    )(cc, w_mod, b_mod)


def _inproj_kernel(x_ref, mods_ref, g_ref, win_ref, wcv_ref, wqt_ref, wk_ref, wvt_ref, qg_ref, kvg_ref,
                   ca_ref, sa_ref, ct_ref, st_ref, *out_refs, latent):
    x = x_ref[0]
    shift = mods_ref[0, 0:1, :]
    scale = mods_ref[0, 1:2, :]
    nx = (_rms(x, g_ref[...]) * (1.0 + scale) + shift).astype(BF16)

    if latent:
        qt_ref, k_ref, vt_ref, gb_ref, u_ref = out_refs
        q_a = _dot(nx, win_ref[:, C_Q:C_KV])
        qnt = _rms(q_a, qg_ref[...]).T.astype(BF16)
        qt = _dot(wqt_ref[...], qnt) * (ATTN_SCALE * LOG2E)
        ct = ct_ref[...]
        st = st_ref[...]
        for h in range(N_HEADS):
            r0 = h * D_QK_PAD
            r1 = r0 + D_NOPE
            r2 = r1 + D_ROPE
            qt_ref[0, 0, r0:r1, :] = qt[r0:r1].astype(BF16)
            qt_ref[0, 0, r1:r2, :] = (qt[r1:r2] * ct + qt[r2:r0 + D_QK_PAD] * st).astype(BF16)
            qt_ref[0, 0, r2:r0 + D_QK_PAD, :] = jnp.zeros((D_ROPE, qt.shape[1]), BF16)
        gb_ref[0] = _dot(nx, wcv_ref[:, :MLA_W]).astype(BF16)
        g_c = _dot(nx, wcv_ref[:, MLA_W:2 * MLA_W])
        hh = _dot(nx, wcv_ref[:, 2 * MLA_W:])
        u_ref[0] = (g_c * hh).astype(BF16)
    else:
        k_ref, vt_ref = out_refs

    kv_a = _dot(nx, win_ref[:, C_KV:C_KR])
    kvn = _rms(kv_a, kvg_ref[...])
    kn = _dot(kvn.astype(BF16), wk_ref[...])
    vt_ref[0, 0] = _dot(wvt_ref[...], kvn.T.astype(BF16)).astype(BF16)
    blk = _dot(nx, win_ref[:, C_KR:C_GB])
    kr = (blk * ca_ref[...] + pltpu.roll(blk, D_ROPE, axis=1) * sa_ref[...]).astype(BF16)
    for h in range(N_HEADS):
        c0 = h * D_QK_PAD
        k_ref[0, :, c0:c0 + D_NOPE] = kn[:, h * D_NOPE:(h + 1) * D_NOPE].astype(BF16)
        k_ref[0, :, c0 + D_NOPE:c0 + D_QK_PAD] = kr


def _inproj(x, mods, mod_row, g, win, wcv, wqt, wk, wvt, qg, kvg, ca, sa, ct, st, *, latent, tm):
    b, s, d = x.shape
    nt = s // tm
    const = lambda a: pl.BlockSpec(a.shape, lambda bi, i: (0,) * a.ndim, pipeline_mode=pl.Buffered(1))
    tok = lambda w: pl.BlockSpec((1, tm, w), lambda bi, i: (bi, i, 0))
    tok_t = lambda w: pl.BlockSpec((1, 1, w, tm), lambda bi, i: (bi, i, 0, 0))
    in_specs = [tok(d),
                pl.BlockSpec((1, N_MOD, d), lambda bi, i: (mod_row(bi), 0, 0)),
                const(g), const(win), const(wcv), const(wqt), const(wk), const(wvt), const(qg), const(kvg),
                pl.BlockSpec((tm, LANES), lambda bi, i: (i, 0)),
                pl.BlockSpec((tm, LANES), lambda bi, i: (i, 0)),
                pl.BlockSpec((D_ROPE, tm), lambda bi, i: (0, i)),
                pl.BlockSpec((D_ROPE, tm), lambda bi, i: (0, i))]
    kq = N_HEADS * D_QK_PAD
    row_major = lambda w: (tok(w), jax.ShapeDtypeStruct((b, s, w), BF16))
    col_major = lambda w: (tok_t(w), jax.ShapeDtypeStruct((b, nt, w, tm), BF16))
    outs = [row_major(kq), col_major(MLA_W)]
    if latent:
        outs = [col_major(kq)] + outs + [row_major(MLA_W), row_major(MLA_W)]
    return pl.pallas_call(
        functools.partial(_inproj_kernel, latent=latent),
        grid=(b, nt),
        in_specs=in_specs,
        out_specs=[o[0] for o in outs],
        out_shape=[o[1] for o in outs],
        compiler_params=pltpu.CompilerParams(dimension_semantics=("parallel", "arbitrary"),
                                             vmem_limit_bytes=VMEM_LIMIT),
        name="inproj_latent" if latent else "inproj_ctx",
    )(x, mods, g, win, wcv, wqt, wk, wvt, qg, kvg, ca, sa, ct, st)


def _attn_kernel(qt_ref, kc_ref, vct_ref, kx_ref, vxt_ref, o_ref, s_ref):
    qt = jnp.concatenate([qt_ref[0, j] for j in range(qt_ref.shape[1])], axis=1)
    tq = qt.shape[1]
    nb = vxt_ref.shape[1]
    tk = kx_ref.shape[1] // nb
    lc = kc_ref.shape[1]

    def scores(k):
        s = _dot(k, qt)
        return s, jnp.max(s, axis=0, keepdims=True)

    def keys(j):
        return kx_ref[0, pl.ds(pl.multiple_of(j * tk, tk), tk), :]

    def absorb(s, mx, vt, carry):
        m, l, acc = carry
        m_new = jnp.maximum(m, mx)
        alpha = jnp.exp2(m - m_new)
        p = jnp.exp2(s - m_new)
        l = alpha * l + jnp.sum(p, axis=0, keepdims=True)
        acc = alpha * acc + _dot(vt, p.astype(BF16))
        return m_new, l, acc

    def pair(cur_rows, cur_vt, j, carry, mx):
        s_n, mx_n = scores(keys(j))
        s_ref[1] = s_n
        carry = absorb(s_ref[0, :cur_rows, :], mx, cur_vt, carry)
        s_a, mx_a = scores(keys(j + 1))
        s_ref[0] = s_a
        carry = absorb(s_ref[1], mx_n, vxt_ref[0, j], carry)
        return carry, mx_a

    init = (jnp.full((1, tq), -jnp.inf, F32), jnp.zeros((1, tq), F32), jnp.zeros((D_V, tq), F32))
    s, mx = scores(kc_ref[0])
    s_ref[0, :lc, :] = s
    state = pair(lc, vct_ref[0, 0], 0, init, mx)

    def body(i, state):
        return pair(tk, vxt_ref[0, 2 * i - 1], 2 * i, *state)

    carry, mx = lax.fori_loop(1, nb // 2, body, state)
    _, l, acc = absorb(s_ref[0], mx, vxt_ref[0, nb - 1], carry)
    o_ref[0] = (acc / l).T.astype(o_ref.dtype)


def _attention(qt, kc, vct, kx, vxt):
    b, nt, _, tk = qt.shape
    s = kx.shape[1]
    lc = kc.shape[1]
    nq = min(ATTN_Q_TILES, nt)
    tq = nq * tk
    assert nt % 2 == 0 and nt % nq == 0 and lc <= tk, \
        "latent chunks are absorbed in pairs after one context chunk"
    return pl.pallas_call(
        _attn_kernel,
        grid=(b, N_HEADS, nt // nq),
        in_specs=[pl.BlockSpec((1, nq, D_QK_PAD, tk), lambda bi, h, i: (bi, i, h, 0)),
                  pl.BlockSpec((1, lc, D_QK_PAD), lambda bi, h, i: (bi, 0, h)),
                  pl.BlockSpec((1, 1, D_V, lc), lambda bi, h, i: (bi, 0, h, 0)),
                  pl.BlockSpec((1, s, D_QK_PAD), lambda bi, h, i: (bi, 0, h)),
                  pl.BlockSpec((1, nt, D_V, tk), lambda bi, h, i: (bi, 0, h, 0))],
        out_specs=pl.BlockSpec((1, tq, D_V), lambda bi, h, i: (bi, i, h)),
        out_shape=jax.ShapeDtypeStruct((b, s, MLA_W), BF16),
        scratch_shapes=[pltpu.VMEM((2, tk, tq), F32)],
        compiler_params=pltpu.CompilerParams(
            dimension_semantics=("parallel", "parallel", "arbitrary"), vmem_limit_bytes=VMEM_LIMIT),
        name="attn",
    )(qt, kc, vct, kx, vxt)


def _route(logits_t, bias):
    e, tm = logits_t.shape
    neg = -jnp.inf
    s = 1.0 / (1.0 + jnp.exp(-logits_t))
    sel = s + bias
    io_g = lax.broadcasted_iota(jnp.int32, (GROUP_SIZE, tm), 0)
    gs = []
    for g in range(N_GROUPS):
        blk = sel[g * GROUP_SIZE:(g + 1) * GROUP_SIZE]
        m1 = jnp.max(blk, axis=0, keepdims=True)
        i1 = jnp.min(jnp.where(blk == m1, io_g, GROUP_SIZE), axis=0, keepdims=True)
        m2 = jnp.max(jnp.where(io_g == i1, neg, blk), axis=0, keepdims=True)
        gs.append(m1 + m2)
    masks = []
    for g in range(N_GROUPS):
        rank = jnp.zeros((1, tm), jnp.int32)
        for o in range(N_GROUPS):
            if o == g:
                continue
            ahead = (gs[o] > gs[g]) | (gs[o] == gs[g]) if o < g else (gs[o] > gs[g])
            rank = rank + ahead.astype(jnp.int32)
        masks.append(jnp.broadcast_to(rank < TOPK_GROUPS, (GROUP_SIZE, tm)))
    emask = jnp.concatenate(masks, axis=0)
    cur = jnp.where(emask, sel, neg)
    io_e = lax.broadcasted_iota(jnp.int32, (e, tm), 0)
    chosen = jnp.zeros((e, tm), jnp.bool_)
    slots = []
    for _ in range(TOP_K):
        m = jnp.max(cur, axis=0, keepdims=True)
        i = jnp.min(jnp.where(cur == m, io_e, e), axis=0, keepdims=True)
        hit = io_e == i
        chosen = chosen | hit
        cur = jnp.where(hit, neg, cur)
        slots.append((hit, i))
    wsum = jnp.sum(jnp.where(chosen, s, 0.0), axis=0, keepdims=True)
    return chosen, jnp.where(chosen, s / wsum * ROUTED_SCALE, 0.0), slots


def _to_token_tiles(x):
    return x.reshape(x.shape[0], x.shape[1] // LANES, LANES)


def _from_token_tiles(x):
    return x.reshape(x.shape[0], x.shape[1] * x.shape[2])


def _post_kernel(om_ref, gb_ref, u_ref, up_ref, un_ref, x_ref, mods_ref, cw_ref, g1_ref, g2_ref,
                 wout_ref, gf_ref, wr_ref, rb_ref, tri_ref,
                 h_ref, xn_ref, ri_ref, rw_ref, cnt_ref, run_ref):
    first = (pl.program_id(0) == 0) & (pl.program_id(1) == 0)

    @pl.when(first)
    def _():
        run_ref[...] = jnp.zeros_like(run_ref)

    i = pl.program_id(1)
    nt = pl.num_programs(1)
    tm = u_ref.shape[1]
    u = u_ref[0].astype(F32)
    prev = jnp.where(i > 0, up_ref[0].astype(F32)[BF16_SUBLANES - 1:BF16_SUBLANES], 0.0)
    nxt = jnp.where(i < nt - 1, un_ref[0].astype(F32)[0:1], 0.0)
    row = lax.broadcasted_iota(jnp.int32, u.shape, 0)
    u_m1 = jnp.where(row == 0, prev, pltpu.roll(u, 1, axis=0))
    u_p1 = jnp.where(row == tm - 1, nxt, pltpu.roll(u, tm - 1, axis=0))
    cw = cw_ref[...]
    y = gb_ref[0].astype(F32) * (cw[0:1] * u_m1 + cw[1:2] * u + cw[2:3] * u_p1)
    o = jnp.concatenate([_rms(om_ref[0].astype(F32), g1_ref[...]).astype(BF16),
                         _rms(y, g2_ref[...]).astype(BF16)], axis=-1)
    ga = mods_ref[0, 2:3, :]
    sf = mods_ref[0, 3:4, :]
    scf = mods_ref[0, 4:5, :]
    h = x_ref[0] + ga * _dot(o, wout_ref[...])
    h_ref[0] = h
    xn = _rms(h, gf_ref[...]) * (1.0 + scf) + sf
    xh, xl = _split_bf16(xn)
    xn_ref[...] = _to_token_tiles(xh)
    hh_hl = _dot(xh, wr_ref[...])
    logits = hh_hl[:, :LANES] + hh_hl[:, LANES:] + _dot(xl, wr_ref[:, :LANES])
    chosen, comb_t, slots = _route(logits.T[:N_EXPERTS], rb_ref[...])
    sel = jnp.where(chosen, 1.0, 0.0)
    before = _dot(sel.astype(BF16), tri_ref[...]) + run_ref[...]
    run_ref[...] += jnp.sum(sel, axis=1, keepdims=True)
    cnt_ref[...] = jnp.broadcast_to(run_ref[...], cnt_ref.shape).astype(jnp.int32)
    pick = lambda hit, v: jnp.sum(jnp.where(hit, v, 0.0), axis=0, keepdims=True)
    ri_ref[...] = jnp.concatenate(
        [idx for _, idx in slots] + [pick(hit, before).astype(jnp.int32) for hit, _ in slots], axis=0)
    rw_ref[...] = jnp.concatenate(
        [pick(hit, comb_t) for hit, _ in slots] + [jnp.zeros((LANES - TOP_K, tm), F32)], axis=0).T


def _post(o_mla, gb, u, x, mods, conv_w, g1, g2, wout, gf, wr, rb, *, tm):
    b, s, d = x.shape
    nt = s // tm
    hb = tm // BF16_SUBLANES
    last = s // BF16_SUBLANES - 1
    tri = jnp.asarray(np.triu(np.ones((tm, tm), np.float32), k=1), BF16)
    const = lambda a: pl.BlockSpec(a.shape, lambda bi, i: (0,) * a.ndim, pipeline_mode=pl.Buffered(1))
    tok = lambda w: pl.BlockSpec((1, tm, w), lambda bi, i: (bi, i, 0))
    flat = lambda w: pl.BlockSpec((tm, w), lambda bi, i: (bi * nt + i, 0))
    in_specs = [tok(MLA_W), tok(MLA_W), tok(MLA_W),
                pl.BlockSpec((1, BF16_SUBLANES, MLA_W), lambda bi, i: (bi, jnp.maximum(i * hb - 1, 0), 0)),
                pl.BlockSpec((1, BF16_SUBLANES, MLA_W), lambda bi, i: (bi, jnp.minimum((i + 1) * hb, last), 0)),
                tok(d),
                pl.BlockSpec((1, N_MOD, d), lambda bi, i: (bi, 0, 0)),
                const(conv_w), const(g1), const(g2), const(wout), const(gf), const(wr),
                const(rb), const(tri)]
    t = b * s
    return pl.pallas_call(
        _post_kernel,
        grid=(b, nt),
        in_specs=in_specs,
        out_specs=[tok(d), pl.BlockSpec((tm, d // LANES, LANES), lambda bi, i: (bi * nt + i, 0, 0)),
                   pl.BlockSpec((2 * TOP_K, tm), lambda bi, i: (0, bi * nt + i)),
                   flat(LANES),
                   pl.BlockSpec((N_EXPERTS, LANES), lambda bi, i: (0, 0))],
        out_shape=[jax.ShapeDtypeStruct((b, s, d), F32),
                   jax.ShapeDtypeStruct((t, d // LANES, LANES), BF16),
                   jax.ShapeDtypeStruct((2 * TOP_K, t), jnp.int32),
                   jax.ShapeDtypeStruct((t, LANES), F32),
                   jax.ShapeDtypeStruct((N_EXPERTS, LANES), jnp.int32)],
        scratch_shapes=[pltpu.VMEM((N_EXPERTS, 1), F32)],
        compiler_params=pltpu.CompilerParams(dimension_semantics=("arbitrary", "arbitrary"),
                                             vmem_limit_bytes=VMEM_LIMIT),
        name="post",
    )(o_mla, gb, u, u, u, x, mods, conv_w, g1, g2, wout, gf, wr, rb, tri)


def _positions_kernel(base_ref, ri_ref, pos_ref):
    e = ri_ref[0:TOP_K, :]
    pos = ri_ref[TOP_K:2 * TOP_K, :]
    for j in range(N_EXPERTS):
        pos = pos + jnp.where(e == j, base_ref[j], 0)
    pos_ref[...] = pos


def _positions(base, ri, *, tm):
    t = ri.shape[1]
    return pl.pallas_call(
        _positions_kernel,
        grid_spec=pltpu.PrefetchScalarGridSpec(
            num_scalar_prefetch=1,
            grid=(t // tm,),
            in_specs=[pl.BlockSpec((2 * TOP_K, tm), lambda i, base: (0, i))],
            out_specs=pl.BlockSpec((TOP_K, tm), lambda i, base: (0, i))),
        out_shape=jax.ShapeDtypeStruct((TOP_K, t), jnp.int32),
        compiler_params=pltpu.CompilerParams(dimension_semantics=("arbitrary",)),
        name="positions",
    )(base, ri)


def _dispatch_kernel(zt_ref, pos_ref, xn_ref, xs_ref, zero_ref, sem):
    tm = xn_ref.shape[0]
    tz = zero_ref.shape[0]

    @pl.when(pl.program_id(0) == 0)
    def _():
        zero_ref[...] = jnp.zeros_like(zero_ref)
        n_fill = zt_ref.shape[0]
        fill = lambda j: pltpu.make_async_copy(
            zero_ref, xs_ref.at[pl.ds(jnp.maximum(zt_ref[j], 0) * tz, tz)], sem)
        for j in range(n_fill):
            pl.when(zt_ref[j] >= 0)(fill(j).start)
        for j in range(n_fill):
            pl.when(zt_ref[j] >= 0)(fill(j).wait)

    for t in range(tm):
        for k in range(TOP_K):
            pltpu.make_async_copy(xn_ref.at[pl.ds(t, 1)], xs_ref.at[pl.ds(pos_ref[k, t], 1)],
                                  sem).start(priority=k % 2)
    for _ in range(TOP_K):
        pltpu.make_async_copy(xn_ref, xs_ref.at[pl.ds(0, tm)], sem).wait()


def _dispatch(zt, pos, xn, *, n_rows, tm, tile):
    t, *row = xn.shape
    return pl.pallas_call(
        _dispatch_kernel,
        grid_spec=pltpu.PrefetchScalarGridSpec(
            num_scalar_prefetch=1,
            grid=(t // tm,),
            in_specs=[pl.BlockSpec((TOP_K, tm), lambda i, zt: (0, i), memory_space=pltpu.SMEM),
                      pl.BlockSpec((tm, *row), lambda i, zt: (i, 0, 0))],
            out_specs=pl.BlockSpec(memory_space=pl.ANY),
            scratch_shapes=[pltpu.VMEM((tile, *row), xn.dtype), pltpu.SemaphoreType.DMA(())]),
        out_shape=jax.ShapeDtypeStruct((n_rows, *row), xn.dtype),
        compiler_params=pltpu.CompilerParams(dimension_semantics=("arbitrary",),
                                             vmem_limit_bytes=VMEM_LIMIT),
        name="dispatch",
    )(zt, pos, xn)


def _expert_kernel(te_ref, tv_ref, xs_ref, wg_ref, wu_ref, wd_ref, ys_ref):
    nv = tv_ref[pl.program_id(0)]

    @pl.when(nv > 0)
    def _():
        x = _from_token_tiles(xs_ref[...])
        rows = lax.broadcasted_iota(jnp.int32, x.shape, 0)
        x = jnp.where(rows < nv, x, jnp.zeros_like(x))
        a = (_silu(_dot(x, wg_ref[0])) * _dot(x, wu_ref[0])).astype(BF16)
        ys_ref[...] = _to_token_tiles(_dot(a, wd_ref[0]).astype(BF16))

    @pl.when(nv == 0)
    def _():
        ys_ref[...] = jnp.zeros_like(ys_ref)


def _experts(te, tv, xs, wg, wu, wd, *, tm):
    n_rows, *row = xs.shape
    per_expert = lambda a: pl.BlockSpec((1,) + a.shape[1:], lambda i, te, tv: (te[i], 0, 0))
    rows = pl.BlockSpec((tm, *row), lambda i, te, tv: (i, 0, 0))
    return pl.pallas_call(
        _expert_kernel,
        grid_spec=pltpu.PrefetchScalarGridSpec(
            num_scalar_prefetch=2,
            grid=(n_rows // tm,),
            in_specs=[rows, per_expert(wg), per_expert(wu), per_expert(wd)],
            out_specs=rows),
        out_shape=jax.ShapeDtypeStruct(xs.shape, xs.dtype),
        compiler_params=pltpu.CompilerParams(dimension_semantics=("arbitrary",),
                                             vmem_limit_bytes=VMEM_LIMIT),
        name="experts",
    )(te, tv, xs, wg, wu, wd)


def _combine_kernel(pos_ref, pos_next_ref, rw_ref, xn_ref, h_ref, mods_ref, wsgu_ref, wsd_ref, gfin_ref,
                    ys_ref, o_ref, buf_ref, sems):
    tm = xn_ref.shape[0]
    d = h_ref.shape[2]
    g = pl.program_id(0) * pl.num_programs(1) + pl.program_id(1)
    n = pl.num_programs(0) * pl.num_programs(1)

    def start_rows(p_ref, t, to):
        for k in range(TOP_K):
            pltpu.make_async_copy(ys_ref.at[pl.ds(p_ref[k, t], 1)],
                                  buf_ref.at[to, k, pl.ds(t, 1)], sems.at[to]).start(priority=k % 2)

    def wait_rows(to):
        for k in range(TOP_K):
            pltpu.make_async_copy(ys_ref.at[pl.ds(0, tm)], buf_ref.at[to, k], sems.at[to]).wait()

    @pl.when(g == 0)
    def _():
        def issue(t, carry):
            start_rows(pos_ref, t, 0)
            return carry
        lax.fori_loop(0, tm, issue, 0, unroll=DMA_UNROLL)

    def tile(slot):
        other = 1 - slot
        xb = _from_token_tiles(xn_ref[...])
        kc, tc = d // COMBINE_SLICES, tm // COMBINE_SLICES
        sgu = None
        for j in range(COMBINE_SLICES):
            part = _dot(xb[:, j * kc:(j + 1) * kc], wsgu_ref[j * kc:(j + 1) * kc, :])
            sgu = part if sgu is None else sgu + part
            for t in range(j * tc, (j + 1) * tc):
                start_rows(pos_next_ref, t, other)
        sf = sgu.shape[1] // 2
        y = _dot((_silu(sgu[:, :sf]) * sgu[:, sf:]).astype(BF16), wsd_ref[...])

        wait_rows(slot)
        rw = rw_ref[...]
        for k in range(TOP_K):
            y = y + rw[:, k:k + 1] * _from_token_tiles(buf_ref[slot, k]).astype(F32)
        gate = mods_ref[0, 5:6, :]
        o_ref[0] = _rms(h_ref[0] + gate * y, gfin_ref[...])

        @pl.when(g == n - 1)
        def _():
            wait_rows(other)

    pl.when(g % 2 == 0)(functools.partial(tile, 0))
    pl.when(g % 2 == 1)(functools.partial(tile, 1))


def _combine(pos, rw, xn, h, mods, wsgu, wsd, gfin, ys, *, tm):
    b, s, d = h.shape
    nt = s // tm
    last = b * nt - 1
    const = lambda a: pl.BlockSpec(a.shape, lambda bi, i: (0,) * a.ndim, pipeline_mode=pl.Buffered(1))
    flat = lambda w: pl.BlockSpec((tm, w), lambda bi, i: (bi * nt + i, 0))
    return pl.pallas_call(
        _combine_kernel,
        grid=(b, nt),
        in_specs=[pl.BlockSpec((TOP_K, tm), lambda bi, i: (0, bi * nt + i), memory_space=pltpu.SMEM),
                  pl.BlockSpec((TOP_K, tm), lambda bi, i: (0, jnp.minimum(bi * nt + i + 1, last)),
                               memory_space=pltpu.SMEM),
                  flat(LANES),
                  pl.BlockSpec((tm,) + xn.shape[1:], lambda bi, i: (bi * nt + i, 0, 0)),
                  pl.BlockSpec((1, tm, d), lambda bi, i: (bi, i, 0)),
                  pl.BlockSpec((1, N_MOD, d), lambda bi, i: (bi, 0, 0)),
                  const(wsgu), const(wsd), const(gfin),
                  pl.BlockSpec(memory_space=pl.ANY)],
        out_specs=pl.BlockSpec((1, tm, d), lambda bi, i: (bi, i, 0)),
        out_shape=jax.ShapeDtypeStruct((b, s, d), F32),
        scratch_shapes=[pltpu.VMEM((2, TOP_K, tm) + ys.shape[1:], ys.dtype),
                        pltpu.SemaphoreType.DMA((2,))],
        compiler_params=pltpu.CompilerParams(dimension_semantics=("arbitrary", "arbitrary"),
                                             vmem_limit_bytes=VMEM_LIMIT),
        name="combine",
    )(pos, pos, rw, xn, h, mods, wsgu, wsd, gfin, ys)


def _rope_tables(s):
    nf = D_ROPE // 4
    pos = np.arange(s)
    inv = ROPE_BASE ** (-np.arange(nf, dtype=np.float64) / nf)
    ar = (pos // GRID_W)[:, None] * inv
    ac = (pos % GRID_W)[:, None] * inv
    c64 = np.concatenate([np.cos(ar), np.cos(ar), np.cos(ac), np.cos(ac)], axis=-1)
    s64 = np.concatenate([-np.sin(ar), np.sin(ar), -np.sin(ac), np.sin(ac)], axis=-1)
    return c64.astype(np.float32), s64.astype(np.float32)


def _swap_pairs(w):
    nf = D_ROPE // 4
    return jnp.concatenate([w[..., nf:2 * nf], w[..., :nf], w[..., 3 * nf:], w[..., 2 * nf:3 * nf]], axis=-1)


def kernel(x, c, ctx, c_ctx, w_mod, b_mod, attn_norm, w_in, q_a_norm, w_q_b, kv_a_norm, w_kv_b, conv_w,
           o_norm_mla, o_norm_conv, w_out, ffn_norm, w_router, router_bias, w_exp_gate, w_exp_up,
           w_exp_down, w_sh_gate, w_sh_up, w_sh_down, final_norm):
    b, s, d = x.shape
    lc = ctx.shape[1]
    assert w_mod.shape[0] == 1, "single trunk layer"
    assert s % GRID_W == 0
    assert lc <= TM_IN, "context keys are processed as a single attention chunk"
    tm_in, tm_post = min(TM_IN, s), min(TM_POST, s)

    wi = w_in[0]
    kr_cols = wi[:, C_KR:C_KR + D_ROPE]
    win = jnp.concatenate([wi[:, :C_KR], kr_cols, _swap_pairs(kr_cols)], axis=-1).astype(BF16)
    wcv = wi[:, C_KR + D_ROPE:].astype(BF16)
    wq3 = w_q_b[0].reshape(Q_RANK, N_HEADS, D_NOPE + D_ROPE)
    wq = jnp.concatenate([wq3, _swap_pairs(wq3[..., D_NOPE:])], axis=-1)
    wqt = wq.reshape(Q_RANK, N_HEADS * D_QK_PAD).T.astype(BF16)
    wkv3 = w_kv_b[0].reshape(KV_RANK, N_HEADS, D_NOPE + D_V)
    wk = wkv3[..., :D_NOPE].reshape(KV_RANK, MLA_W).astype(BF16)
    wvt = wkv3[..., D_NOPE:].reshape(KV_RANK, MLA_W).T.astype(BF16)
    wr = jnp.pad(w_router[0], ((0, 0), (0, LANES - N_EXPERTS)))
    wr = jnp.concatenate(_split_bf16(wr), axis=-1)
    wg, wu = w_exp_gate[0].astype(BF16), w_exp_up[0].astype(BF16)
    wd = w_exp_down[0].astype(BF16)
    wsgu = jnp.concatenate([w_sh_gate[0], w_sh_up[0]], axis=-1).astype(BF16)
    wsd = w_sh_down[0].astype(BF16)
    row = lambda v: v.reshape(1, -1)

    n_rows = -(-(b + 1) // 8) * 8
    cc = jnp.zeros((n_rows, d), F32).at[:b].set(c).at[b].set(c_ctx)
    mods = _mods(cc, w_mod[0], row(b_mod[0])).reshape(n_rows, N_MOD, d)

    c64, s64 = _rope_tables(s)
    pad = lambda t: np.pad(t, ((0, 0), (0, LANES - D_ROPE)))
    ones_c, zeros_c = np.ones((lc, D_ROPE), np.float32), np.zeros((lc, D_ROPE), np.float32)
    tail = (wqt, wk, wvt, row(q_a_norm[0]), row(kv_a_norm[0]))
    common = (row(attn_norm[0]), win, wcv) + tail
    ctx_only = (row(attn_norm[0]), win, wcv[:, :LANES]) + tail
    kc, vct = _inproj(ctx, mods, lambda bi: b, *ctx_only, pad(ones_c), pad(zeros_c), ones_c.T, zeros_c.T,
                      latent=False, tm=min(TM_IN, lc))
    qt, kx, vxt, gb, u = _inproj(x, mods, lambda bi: bi, *common, pad(c64), pad(s64), c64.T, s64.T,
                                 latent=True, tm=tm_in)

    o_mla = _attention(qt, kc, vct, kx, vxt)

    h, xn, ri, rw, cnt = _post(o_mla, gb, u, x, mods, conv_w[0], row(o_norm_mla[0]), row(o_norm_conv[0]),
                               w_out[0].astype(BF16), row(ffn_norm[0]), wr,
                               router_bias[0].reshape(-1, 1), tm=tm_post)

    tmx = TM_EXPERT
    n_tiles = pl.cdiv(b * s * TOP_K, tmx) + N_EXPERTS
    counts = cnt[:, 0]
    tiles_per = (counts + tmx - 1) // tmx
    tile_end = jnp.cumsum(tiles_per)
    tile_start = tile_end - tiles_per
    base = (tile_start * tmx).astype(jnp.int32)
    tile_ids = jnp.arange(n_tiles, dtype=jnp.int32)
    te = jnp.minimum(jnp.sum(tile_end[None, :] <= tile_ids[:, None], axis=1), N_EXPERTS - 1)
    own = te[:, None] == jnp.arange(N_EXPERTS)[None, :]
    left = jnp.sum(jnp.where(own, counts[None, :] - (tile_ids[:, None] - tile_start[None, :]) * tmx, 0),
                   axis=1)
    tv = jnp.where(tile_ids < tile_end[-1], jnp.clip(left, 0, tmx), 0).astype(jnp.int32)
    tail = tile_end[-1] + jnp.arange(N_EXPERTS)
    zt = jnp.concatenate([jnp.where(tiles_per > 0, tile_end - 1, -1),
                          jnp.where(tail < n_tiles, tail, -1)]).astype(jnp.int32)

    pos = _positions(base, ri, tm=min(TM_POSITIONS, b * s))
    xs = _dispatch(zt, pos, xn, n_rows=n_tiles * tmx, tm=min(TM_DISPATCH, s), tile=tmx)
    ys = _experts(te.astype(jnp.int32), tv, xs, wg, wu, wd, tm=tmx)
    return _combine(pos, rw, xn, h, mods, wsgu, wsd, row(final_norm), ys, tm=min(TM_COMBINE, s))
```

```python
import functools
import math

import jax
import jax.numpy as jnp
import numpy as np
from jax import lax
from jax.experimental import pallas as pl
from jax.experimental.pallas import tpu as pltpu

F32 = jnp.float32
BF16 = jnp.bfloat16

N_HEADS = 8
D_NOPE = 128
D_ROPE = 64
D_V = 128
D_QK_PAD = 256
Q_RANK = 512
KV_RANK = 256
MLA_W = N_HEADS * D_V
GRID_W = 64
ROPE_BASE = 10000.0
ATTN_SCALE = (D_NOPE + D_ROPE) ** -0.5
LOG2E = math.log2(math.e)
N_EXPERTS = 32
N_GROUPS = 4
GROUP_SIZE = N_EXPERTS // N_GROUPS
TOPK_GROUPS = 2
TOP_K = 4
ROUTED_SCALE = 2.5
N_MOD = 6
EPS = 1e-6

C_Q = 0
C_KV = Q_RANK
C_KR = C_KV + KV_RANK
C_GB = C_KR + 128

LANES = 128
BF16_SUBLANES = 16
VMEM_LIMIT = 56 * 1024 * 1024

TM_IN = 512
ATTN_Q_TILES = 8
TM_POST = 512
TM_POSITIONS = 4096
TM_DISPATCH = 256
TM_EXPERT = 512
TM_COMBINE = 256
TN_MODS = 1024
DMA_UNROLL = 8
COMBINE_SLICES = 8


def _rms(x, g):
    return x * lax.rsqrt(jnp.mean(x * x, axis=-1, keepdims=True) + EPS) * g


def _silu(x):
    return x / (1.0 + jnp.exp(-x))


def _split_bf16(x):
    hi = x.astype(BF16)
    lo = (x - hi.astype(F32)).astype(BF16)
    return hi, lo


def _dot(a, b):
    return jnp.dot(a, b, preferred_element_type=F32)


def _mods_kernel(a_ref, w_ref, b_ref, o_ref):
    a = _silu(a_ref[...])
    ah, al = _split_bf16(a)
    wh, wl = _split_bf16(w_ref[...])
    o_ref[...] = _dot(ah, wh) + _dot(al, wh) + _dot(ah, wl) + b_ref[...]


def _mods(cc, w_mod, b_mod):
    rows, d = cc.shape
    n = w_mod.shape[1]
    return pl.pallas_call(
        _mods_kernel,
        grid=(n // TN_MODS,),
        in_specs=[pl.BlockSpec((rows, d), lambda j: (0, 0)),
                  pl.BlockSpec((d, TN_MODS), lambda j: (0, j)),
                  pl.BlockSpec((1, TN_MODS), lambda j: (0, j))],
        out_specs=pl.BlockSpec((rows, TN_MODS), lambda j: (0, j)),
        out_shape=jax.ShapeDtypeStruct((rows, n), F32),
        compiler_params=pltpu.CompilerParams(dimension_semantics=("arbitrary",),
                                             vmem_limit_bytes=VMEM_LIMIT),
        name="mods",
    )(cc, w_mod, b_mod)


def _inproj_kernel(x_ref, mods_ref, g_ref, win_ref, wcv_ref, wqt_ref, wk_ref, wvt_ref, qg_ref, kvg_ref,
                   ca_ref, sa_ref, ct_ref, st_ref, *out_refs, latent):
    x = x_ref[0]
    shift = mods_ref[0, 0:1, :]
    scale = mods_ref[0, 1:2, :]
    nx = (_rms(x, g_ref[...]) * (1.0 + scale) + shift).astype(BF16)

    if latent:
        qt_ref, k_ref, vt_ref, gb_ref, u_ref = out_refs
        q_a = _dot(nx, win_ref[:, C_Q:C_KV])
        qnt = _rms(q_a, qg_ref[...]).T.astype(BF16)
        qt = _dot(wqt_ref[...], qnt) * (ATTN_SCALE * LOG2E)
        ct = ct_ref[...]
        st = st_ref[...]
        for h in range(N_HEADS):
            r0 = h * D_QK_PAD
            r1 = r0 + D_NOPE
            r2 = r1 + D_ROPE
            qt_ref[0, 0, r0:r1, :] = qt[r0:r1].astype(BF16)
            qt_ref[0, 0, r1:r2, :] = (qt[r1:r2] * ct + qt[r2:r0 + D_QK_PAD] * st).astype(BF16)
            qt_ref[0, 0, r2:r0 + D_QK_PAD, :] = jnp.zeros((D_ROPE, qt.shape[1]), BF16)
        gb_ref[0] = _dot(nx, wcv_ref[:, :MLA_W]).astype(BF16)
        g_c = _dot(nx, wcv_ref[:, MLA_W:2 * MLA_W])
        hh = _dot(nx, wcv_ref[:, 2 * MLA_W:])
        u_ref[0] = (g_c * hh).astype(BF16)
    else:
        k_ref, vt_ref = out_refs

    kv_a = _dot(nx, win_ref[:, C_KV:C_KR])
    kvn = _rms(kv_a, kvg_ref[...])
    kn = _dot(kvn.astype(BF16), wk_ref[...])
    vt_ref[0, 0] = _dot(wvt_ref[...], kvn.T.astype(BF16)).astype(BF16)
    blk = _dot(nx, win_ref[:, C_KR:C_GB])
    kr = (blk * ca_ref[...] + pltpu.roll(blk, D_ROPE, axis=1) * sa_ref[...]).astype(BF16)
    for h in range(N_HEADS):
        c0 = h * D_QK_PAD
        k_ref[0, :, c0:c0 + D_NOPE] = kn[:, h * D_NOPE:(h + 1) * D_NOPE].astype(BF16)
        k_ref[0, :, c0 + D_NOPE:c0 + D_QK_PAD] = kr


def _inproj(x, mods, mod_row, g, win, wcv, wqt, wk, wvt, qg, kvg, ca, sa, ct, st, *, latent, tm):
    b, s, d = x.shape
    nt = s // tm
    const = lambda a: pl.BlockSpec(a.shape, lambda bi, i: (0,) * a.ndim, pipeline_mode=pl.Buffered(1))
    tok = lambda w: pl.BlockSpec((1, tm, w), lambda bi, i: (bi, i, 0))
    tok_t = lambda w: pl.BlockSpec((1, 1, w, tm), lambda bi, i: (bi, i, 0, 0))
    in_specs = [tok(d),
                pl.BlockSpec((1, N_MOD, d), lambda bi, i: (mod_row(bi), 0, 0)),
                const(g), const(win), const(wcv), const(wqt), const(wk), const(wvt), const(qg), const(kvg),
                pl.BlockSpec((tm, LANES), lambda bi, i: (i, 0)),
                pl.BlockSpec((tm, LANES), lambda bi, i: (i, 0)),
                pl.BlockSpec((D_ROPE, tm), lambda bi, i: (0, i)),
                pl.BlockSpec((D_ROPE, tm), lambda bi, i: (0, i))]
    kq = N_HEADS * D_QK_PAD
    row_major = lambda w: (tok(w), jax.ShapeDtypeStruct((b, s, w), BF16))
    col_major = lambda w: (tok_t(w), jax.ShapeDtypeStruct((b, nt, w, tm), BF16))
    outs = [row_major(kq), col_major(MLA_W)]
    if latent:
        outs = [col_major(kq)] + outs + [row_major(MLA_W), row_major(MLA_W)]
    return pl.pallas_call(
        functools.partial(_inproj_kernel, latent=latent),
        grid=(b, nt),
        in_specs=in_specs,
        out_specs=[o[0] for o in outs],
        out_shape=[o[1] for o in outs],
        compiler_params=pltpu.CompilerParams(dimension_semantics=("parallel", "arbitrary"),
                                             vmem_limit_bytes=VMEM_LIMIT),
        name="inproj_latent" if latent else "inproj_ctx",
    )(x, mods, g, win, wcv, wqt, wk, wvt, qg, kvg, ca, sa, ct, st)


def _attn_kernel(qt_ref, kc_ref, vct_ref, kx_ref, vxt_ref, o_ref, s_ref):
    qt = jnp.concatenate([qt_ref[0, j] for j in range(qt_ref.shape[1])], axis=1)
    tq = qt.shape[1]
    nb = vxt_ref.shape[1]
    tk = kx_ref.shape[1] // nb
    lc = kc_ref.shape[1]

    def scores(k):
        s = _dot(k, qt)
        return s, jnp.max(s, axis=0, keepdims=True)

    def keys(j):
        return kx_ref[0, pl.ds(pl.multiple_of(j * tk, tk), tk), :]

    def absorb(s, mx, vt, carry):
        m, l, acc = carry
        m_new = jnp.maximum(m, mx)
        alpha = jnp.exp2(m - m_new)
        p = jnp.exp2(s - m_new)
        l = alpha * l + jnp.sum(p, axis=0, keepdims=True)
        acc = alpha * acc + _dot(vt, p.astype(BF16))
        return m_new, l, acc

    def pair(cur_rows, cur_vt, j, carry, mx):
        s_n, mx_n = scores(keys(j))
        s_ref[1] = s_n
        carry = absorb(s_ref[0, :cur_rows, :], mx, cur_vt, carry)
        s_a, mx_a = scores(keys(j + 1))
        s_ref[0] = s_a
        carry = absorb(s_ref[1], mx_n, vxt_ref[0, j], carry)
        return carry, mx_a

    init = (jnp.full((1, tq), -jnp.inf, F32), jnp.zeros((1, tq), F32), jnp.zeros((D_V, tq), F32))
    s, mx = scores(kc_ref[0])
    s_ref[0, :lc, :] = s
    state = pair(lc, vct_ref[0, 0], 0, init, mx)

    def body(i, state):
        return pair(tk, vxt_ref[0, 2 * i - 1], 2 * i, *state)

    carry, mx = lax.fori_loop(1, nb // 2, body, state)
    _, l, acc = absorb(s_ref[0], mx, vxt_ref[0, nb - 1], carry)
    o_ref[0] = (acc / l).T.astype(o_ref.dtype)


def _attention(qt, kc, vct, kx, vxt):
    b, nt, _, tk = qt.shape
    s = kx.shape[1]
    lc = kc.shape[1]
    nq = min(ATTN_Q_TILES, nt)
    tq = nq * tk
    assert nt % 2 == 0 and nt % nq == 0 and lc <= tk, \
        "latent chunks are absorbed in pairs after one context chunk"
    return pl.pallas_call(
        _attn_kernel,
        grid=(b, N_HEADS, nt // nq),
        in_specs=[pl.BlockSpec((1, nq, D_QK_PAD, tk), lambda bi, h, i: (bi, i, h, 0)),
                  pl.BlockSpec((1, lc, D_QK_PAD), lambda bi, h, i: (bi, 0, h)),
                  pl.BlockSpec((1, 1, D_V, lc), lambda bi, h, i: (bi, 0, h, 0)),
                  pl.BlockSpec((1, s, D_QK_PAD), lambda bi, h, i: (bi, 0, h)),
                  pl.BlockSpec((1, nt, D_V, tk), lambda bi, h, i: (bi, 0, h, 0))],
        out_specs=pl.BlockSpec((1, tq, D_V), lambda bi, h, i: (bi, i, h)),
        out_shape=jax.ShapeDtypeStruct((b, s, MLA_W), BF16),
        scratch_shapes=[pltpu.VMEM((2, tk, tq), F32)],
        compiler_params=pltpu.CompilerParams(
            dimension_semantics=("parallel", "parallel", "arbitrary"), vmem_limit_bytes=VMEM_LIMIT),
        name="attn",
    )(qt, kc, vct, kx, vxt)


def _route(logits_t, bias):
    e, tm = logits_t.shape
    neg = -jnp.inf
    s = 1.0 / (1.0 + jnp.exp(-logits_t))
    sel = s + bias
    io_g = lax.broadcasted_iota(jnp.int32, (GROUP_SIZE, tm), 0)
    gs = []
    for g in range(N_GROUPS):
        blk = sel[g * GROUP_SIZE:(g + 1) * GROUP_SIZE]
        m1 = jnp.max(blk, axis=0, keepdims=True)
        i1 = jnp.min(jnp.where(blk == m1, io_g, GROUP_SIZE), axis=0, keepdims=True)
        m2 = jnp.max(jnp.where(io_g == i1, neg, blk), axis=0, keepdims=True)
        gs.append(m1 + m2)
    masks = []
    for g in range(N_GROUPS):
        rank = jnp.zeros((1, tm), jnp.int32)
        for o in range(N_GROUPS):
            if o == g:
                continue
            ahead = (gs[o] > gs[g]) | (gs[o] == gs[g]) if o < g else (gs[o] > gs[g])
            rank = rank + ahead.astype(jnp.int32)
        masks.append(jnp.broadcast_to(rank < TOPK_GROUPS, (GROUP_SIZE, tm)))
    emask = jnp.concatenate(masks, axis=0)
    cur = jnp.where(emask, sel, neg)
    io_e = lax.broadcasted_iota(jnp.int32, (e, tm), 0)
    chosen = jnp.zeros((e, tm), jnp.bool_)
    slots = []
    for _ in range(TOP_K):
        m = jnp.max(cur, axis=0, keepdims=True)
        i = jnp.min(jnp.where(cur == m, io_e, e), axis=0, keepdims=True)
        hit = io_e == i
        chosen = chosen | hit
        cur = jnp.where(hit, neg, cur)
        slots.append((hit, i))
    wsum = jnp.sum(jnp.where(chosen, s, 0.0), axis=0, keepdims=True)
    return chosen, jnp.where(chosen, s / wsum * ROUTED_SCALE, 0.0), slots


def _to_token_tiles(x):
    return x.reshape(x.shape[0], x.shape[1] // LANES, LANES)


def _from_token_tiles(x):
    return x.reshape(x.shape[0], x.shape[1] * x.shape[2])


def _post_kernel(om_ref, gb_ref, u_ref, up_ref, un_ref, x_ref, mods_ref, cw_ref, g1_ref, g2_ref,
                 wout_ref, gf_ref, wr_ref, rb_ref, tri_ref,
                 h_ref, xn_ref, ri_ref, rw_ref, cnt_ref, run_ref):
    first = (pl.program_id(0) == 0) & (pl.program_id(1) == 0)

    @pl.when(first)
    def _():
        run_ref[...] = jnp.zeros_like(run_ref)

    i = pl.program_id(1)
    nt = pl.num_programs(1)
    tm = u_ref.shape[1]
    u = u_ref[0].astype(F32)
    prev = jnp.where(i > 0, up_ref[0].astype(F32)[BF16_SUBLANES - 1:BF16_SUBLANES], 0.0)
    nxt = jnp.where(i < nt - 1, un_ref[0].astype(F32)[0:1], 0.0)
    row = lax.broadcasted_iota(jnp.int32, u.shape, 0)
    u_m1 = jnp.where(row == 0, prev, pltpu.roll(u, 1, axis=0))
    u_p1 = jnp.where(row == tm - 1, nxt, pltpu.roll(u, tm - 1, axis=0))
    cw = cw_ref[...]
    y = gb_ref[0].astype(F32) * (cw[0:1] * u_m1 + cw[1:2] * u + cw[2:3] * u_p1)
    o = jnp.concatenate([_rms(om_ref[0].astype(F32), g1_ref[...]).astype(BF16),
                         _rms(y, g2_ref[...]).astype(BF16)], axis=-1)
    ga = mods_ref[0, 2:3, :]
    sf = mods_ref[0, 3:4, :]
    scf = mods_ref[0, 4:5, :]
    h = x_ref[0] + ga * _dot(o, wout_ref[...])
    h_ref[0] = h
    xn = _rms(h, gf_ref[...]) * (1.0 + scf) + sf
    xh, xl = _split_bf16(xn)
    xn_ref[...] = _to_token_tiles(xh)
    hh_hl = _dot(xh, wr_ref[...])
    logits = hh_hl[:, :LANES] + hh_hl[:, LANES:] + _dot(xl, wr_ref[:, :LANES])
    chosen, comb_t, slots = _route(logits.T[:N_EXPERTS], rb_ref[...])
    sel = jnp.where(chosen, 1.0, 0.0)
    before = _dot(sel.astype(BF16), tri_ref[...]) + run_ref[...]
    run_ref[...] += jnp.sum(sel, axis=1, keepdims=True)
    cnt_ref[...] = jnp.broadcast_to(run_ref[...], cnt_ref.shape).astype(jnp.int32)
    pick = lambda hit, v: jnp.sum(jnp.where(hit, v, 0.0), axis=0, keepdims=True)
    ri_ref[...] = jnp.concatenate(
        [idx for _, idx in slots] + [pick(hit, before).astype(jnp.int32) for hit, _ in slots], axis=0)
    rw_ref[...] = jnp.concatenate(
        [pick(hit, comb_t) for hit, _ in slots] + [jnp.zeros((LANES - TOP_K, tm), F32)], axis=0).T


def _post(o_mla, gb, u, x, mods, conv_w, g1, g2, wout, gf, wr, rb, *, tm):
    b, s, d = x.shape
    nt = s // tm
    hb = tm // BF16_SUBLANES
    last = s // BF16_SUBLANES - 1
    tri = jnp.asarray(np.triu(np.ones((tm, tm), np.float32), k=1), BF16)
    const = lambda a: pl.BlockSpec(a.shape, lambda bi, i: (0,) * a.ndim, pipeline_mode=pl.Buffered(1))
    tok = lambda w: pl.BlockSpec((1, tm, w), lambda bi, i: (bi, i, 0))
    flat = lambda w: pl.BlockSpec((tm, w), lambda bi, i: (bi * nt + i, 0))
    in_specs = [tok(MLA_W), tok(MLA_W), tok(MLA_W),
                pl.BlockSpec((1, BF16_SUBLANES, MLA_W), lambda bi, i: (bi, jnp.maximum(i * hb - 1, 0), 0)),
                pl.BlockSpec((1, BF16_SUBLANES, MLA_W), lambda bi, i: (bi, jnp.minimum((i + 1) * hb, last), 0)),
                tok(d),
                pl.BlockSpec((1, N_MOD, d), lambda bi, i: (bi, 0, 0)),
                const(conv_w), const(g1), const(g2), const(wout), const(gf), const(wr),
                const(rb), const(tri)]
    t = b * s
    return pl.pallas_call(
        _post_kernel,
        grid=(b, nt),
        in_specs=in_specs,
        out_specs=[tok(d), pl.BlockSpec((tm, d // LANES, LANES), lambda bi, i: (bi * nt + i, 0, 0)),
                   pl.BlockSpec((2 * TOP_K, tm), lambda bi, i: (0, bi * nt + i)),
                   flat(LANES),
                   pl.BlockSpec((N_EXPERTS, LANES), lambda bi, i: (0, 0))],
        out_shape=[jax.ShapeDtypeStruct((b, s, d), F32),
                   jax.ShapeDtypeStruct((t, d // LANES, LANES), BF16),
                   jax.ShapeDtypeStruct((2 * TOP_K, t), jnp.int32),
                   jax.ShapeDtypeStruct((t, LANES), F32),
                   jax.ShapeDtypeStruct((N_EXPERTS, LANES), jnp.int32)],
        scratch_shapes=[pltpu.VMEM((N_EXPERTS, 1), F32)],
        compiler_params=pltpu.CompilerParams(dimension_semantics=("arbitrary", "arbitrary"),
                                             vmem_limit_bytes=VMEM_LIMIT),
        name="post",
    )(o_mla, gb, u, u, u, x, mods, conv_w, g1, g2, wout, gf, wr, rb, tri)


def _positions_kernel(base_ref, ri_ref, pos_ref):
    e = ri_ref[0:TOP_K, :]
    pos = ri_ref[TOP_K:2 * TOP_K, :]
    for j in range(N_EXPERTS):
        pos = pos + jnp.where(e == j, base_ref[j], 0)
    pos_ref[...] = pos


def _positions(base, ri, *, tm):
    t = ri.shape[1]
    return pl.pallas_call(
        _positions_kernel,
        grid_spec=pltpu.PrefetchScalarGridSpec(
            num_scalar_prefetch=1,
            grid=(t // tm,),
            in_specs=[pl.BlockSpec((2 * TOP_K, tm), lambda i, base: (0, i))],
            out_specs=pl.BlockSpec((TOP_K, tm), lambda i, base: (0, i))),
        out_shape=jax.ShapeDtypeStruct((TOP_K, t), jnp.int32),
        compiler_params=pltpu.CompilerParams(dimension_semantics=("arbitrary",)),
        name="positions",
    )(base, ri)


def _dispatch_kernel(zt_ref, pos_ref, xn_ref, xs_ref, zero_ref, sem):
    tm = xn_ref.shape[0]
    tz = zero_ref.shape[0]

    @pl.when(pl.program_id(0) == 0)
    def _():
        zero_ref[...] = jnp.zeros_like(zero_ref)
        n_fill = zt_ref.shape[0]
        fill = lambda j: pltpu.make_async_copy(
            zero_ref, xs_ref.at[pl.ds(jnp.maximum(zt_ref[j], 0) * tz, tz)], sem)
        for j in range(n_fill):
            pl.when(zt_ref[j] >= 0)(fill(j).start)
        for j in range(n_fill):
            pl.when(zt_ref[j] >= 0)(fill(j).wait)

    for t in range(tm):
        for k in range(TOP_K):
            pltpu.make_async_copy(xn_ref.at[pl.ds(t, 1)], xs_ref.at[pl.ds(pos_ref[k, t], 1)],
                                  sem).start(priority=k % 2)
    for _ in range(TOP_K):
        pltpu.make_async_copy(xn_ref, xs_ref.at[pl.ds(0, tm)], sem).wait()


def _dispatch(zt, pos, xn, *, n_rows, tm, tile):
    t, *row = xn.shape
    return pl.pallas_call(
        _dispatch_kernel,
        grid_spec=pltpu.PrefetchScalarGridSpec(
            num_scalar_prefetch=1,
            grid=(t // tm,),
            in_specs=[pl.BlockSpec((TOP_K, tm), lambda i, zt: (0, i), memory_space=pltpu.SMEM),
                      pl.BlockSpec((tm, *row), lambda i, zt: (i, 0, 0))],
            out_specs=pl.BlockSpec(memory_space=pl.ANY),
            scratch_shapes=[pltpu.VMEM((tile, *row), xn.dtype), pltpu.SemaphoreType.DMA(())]),
        out_shape=jax.ShapeDtypeStruct((n_rows, *row), xn.dtype),
        compiler_params=pltpu.CompilerParams(dimension_semantics=("arbitrary",),
                                             vmem_limit_bytes=VMEM_LIMIT),
        name="dispatch",
    )(zt, pos, xn)


def _expert_kernel(te_ref, tv_ref, xs_ref, wg_ref, wu_ref, wd_ref, ys_ref):
    nv = tv_ref[pl.program_id(0)]

    @pl.when(nv > 0)
    def _():
        x = _from_token_tiles(xs_ref[...])
        rows = lax.broadcasted_iota(jnp.int32, x.shape, 0)
        x = jnp.where(rows < nv, x, jnp.zeros_like(x))
        a = (_silu(_dot(x, wg_ref[0])) * _dot(x, wu_ref[0])).astype(BF16)
        ys_ref[...] = _to_token_tiles(_dot(a, wd_ref[0]).astype(BF16))

    @pl.when(nv == 0)
    def _():
        ys_ref[...] = jnp.zeros_like(ys_ref)


def _experts(te, tv, xs, wg, wu, wd, *, tm):
    n_rows, *row = xs.shape
    per_expert = lambda a: pl.BlockSpec((1,) + a.shape[1:], lambda i, te, tv: (te[i], 0, 0))
    rows = pl.BlockSpec((tm, *row), lambda i, te, tv: (i, 0, 0))
    return pl.pallas_call(
        _expert_kernel,
        grid_spec=pltpu.PrefetchScalarGridSpec(
            num_scalar_prefetch=2,
            grid=(n_rows // tm,),
            in_specs=[rows, per_expert(wg), per_expert(wu), per_expert(wd)],
            out_specs=rows),
        out_shape=jax.ShapeDtypeStruct(xs.shape, xs.dtype),
        compiler_params=pltpu.CompilerParams(dimension_semantics=("arbitrary",),
                                             vmem_limit_bytes=VMEM_LIMIT),
        name="experts",
    )(te, tv, xs, wg, wu, wd)


def _combine_kernel(pos_ref, pos_next_ref, rw_ref, xn_ref, h_ref, mods_ref, wsgu_ref, wsd_ref, gfin_ref,
                    ys_ref, o_ref, buf_ref, sems):
    tm = xn_ref.shape[0]
    d = h_ref.shape[2]
    g = pl.program_id(0) * pl.num_programs(1) + pl.program_id(1)
    n = pl.num_programs(0) * pl.num_programs(1)

    def start_rows(p_ref, t, to):
        for k in range(TOP_K):
            pltpu.make_async_copy(ys_ref.at[pl.ds(p_ref[k, t], 1)],
                                  buf_ref.at[to, k, pl.ds(t, 1)], sems.at[to]).start(priority=k % 2)

    def wait_rows(to):
        for k in range(TOP_K):
            pltpu.make_async_copy(ys_ref.at[pl.ds(0, tm)], buf_ref.at[to, k], sems.at[to]).wait()

    @pl.when(g == 0)
    def _():
        def issue(t, carry):
            start_rows(pos_ref, t, 0)
            return carry
        lax.fori_loop(0, tm, issue, 0, unroll=DMA_UNROLL)

    def tile(slot):
        other = 1 - slot
        wait_rows(slot)
        rw = rw_ref[...]
        xb = _from_token_tiles(xn_ref[...])
        kc, tc = d // COMBINE_SLICES, tm // COMBINE_SLICES
        sgu = routed = None
        for j in range(COMBINE_SLICES):
            part = _dot(xb[:, j * kc:(j + 1) * kc], wsgu_ref[j * kc:(j + 1) * kc, :])
            sgu = part if sgu is None else sgu + part
            if j < TOP_K:
                term = rw[:, j:j + 1] * _from_token_tiles(buf_ref[slot, j]).astype(F32)
                routed = term if routed is None else routed + term
            for t in range(j * tc, (j + 1) * tc):
                start_rows(pos_next_ref, t, other)
        sf = sgu.shape[1] // 2
        y = routed + _dot((_silu(sgu[:, :sf]) * sgu[:, sf:]).astype(BF16), wsd_ref[...])
        gate = mods_ref[0, 5:6, :]
        o_ref[0] = _rms(h_ref[0] + gate * y, gfin_ref[...])

        @pl.when(g == n - 1)
        def _():
            wait_rows(other)

    pl.when(g % 2 == 0)(functools.partial(tile, 0))
    pl.when(g % 2 == 1)(functools.partial(tile, 1))


def _combine(pos, rw, xn, h, mods, wsgu, wsd, gfin, ys, *, tm):
    b, s, d = h.shape
    nt = s // tm
    last = b * nt - 1
    const = lambda a: pl.BlockSpec(a.shape, lambda bi, i: (0,) * a.ndim, pipeline_mode=pl.Buffered(1))
    flat = lambda w: pl.BlockSpec((tm, w), lambda bi, i: (bi * nt + i, 0))
    return pl.pallas_call(
        _combine_kernel,
        grid=(b, nt),
        in_specs=[pl.BlockSpec((TOP_K, tm), lambda bi, i: (0, bi * nt + i), memory_space=pltpu.SMEM),
                  pl.BlockSpec((TOP_K, tm), lambda bi, i: (0, jnp.minimum(bi * nt + i + 1, last)),
                               memory_space=pltpu.SMEM),
                  flat(LANES),
                  pl.BlockSpec((tm,) + xn.shape[1:], lambda bi, i: (bi * nt + i, 0, 0)),
                  pl.BlockSpec((1, tm, d), lambda bi, i: (bi, i, 0)),
                  pl.BlockSpec((1, N_MOD, d), lambda bi, i: (bi, 0, 0)),
                  const(wsgu), const(wsd), const(gfin),
                  pl.BlockSpec(memory_space=pl.ANY)],
        out_specs=pl.BlockSpec((1, tm, d), lambda bi, i: (bi, i, 0)),
        out_shape=jax.ShapeDtypeStruct((b, s, d), F32),
        scratch_shapes=[pltpu.VMEM((2, TOP_K, tm) + ys.shape[1:], ys.dtype),
                        pltpu.SemaphoreType.DMA((2,))],
        compiler_params=pltpu.CompilerParams(dimension_semantics=("arbitrary", "arbitrary"),
                                             vmem_limit_bytes=VMEM_LIMIT),
        name="combine",
    )(pos, pos, rw, xn, h, mods, wsgu, wsd, gfin, ys)


def _rope_tables(s):
    nf = D_ROPE // 4
    pos = np.arange(s)
    inv = ROPE_BASE ** (-np.arange(nf, dtype=np.float64) / nf)
    ar = (pos // GRID_W)[:, None] * inv
    ac = (pos % GRID_W)[:, None] * inv
    c64 = np.concatenate([np.cos(ar), np.cos(ar), np.cos(ac), np.cos(ac)], axis=-1)
    s64 = np.concatenate([-np.sin(ar), np.sin(ar), -np.sin(ac), np.sin(ac)], axis=-1)
    return c64.astype(np.float32), s64.astype(np.float32)


def _swap_pairs(w):
    nf = D_ROPE // 4
    return jnp.concatenate([w[..., nf:2 * nf], w[..., :nf], w[..., 3 * nf:], w[..., 2 * nf:3 * nf]], axis=-1)


def kernel(x, c, ctx, c_ctx, w_mod, b_mod, attn_norm, w_in, q_a_norm, w_q_b, kv_a_norm, w_kv_b, conv_w,
           o_norm_mla, o_norm_conv, w_out, ffn_norm, w_router, router_bias, w_exp_gate, w_exp_up,
           w_exp_down, w_sh_gate, w_sh_up, w_sh_down, final_norm):
    b, s, d = x.shape
    lc = ctx.shape[1]
    assert w_mod.shape[0] == 1, "single trunk layer"
    assert s % GRID_W == 0
    assert lc <= TM_IN, "context keys are processed as a single attention chunk"
    tm_in, tm_post = min(TM_IN, s), min(TM_POST, s)

    wi = w_in[0]
    kr_cols = wi[:, C_KR:C_KR + D_ROPE]
    win = jnp.concatenate([wi[:, :C_KR], kr_cols, _swap_pairs(kr_cols)], axis=-1).astype(BF16)
    wcv = wi[:, C_KR + D_ROPE:].astype(BF16)
    wq3 = w_q_b[0].reshape(Q_RANK, N_HEADS, D_NOPE + D_ROPE)
    wq = jnp.concatenate([wq3, _swap_pairs(wq3[..., D_NOPE:])], axis=-1)
    wqt = wq.reshape(Q_RANK, N_HEADS * D_QK_PAD).T.astype(BF16)
    wkv3 = w_kv_b[0].reshape(KV_RANK, N_HEADS, D_NOPE + D_V)
    wk = wkv3[..., :D_NOPE].reshape(KV_RANK, MLA_W).astype(BF16)
    wvt = wkv3[..., D_NOPE:].reshape(KV_RANK, MLA_W).T.astype(BF16)
    wr = jnp.pad(w_router[0], ((0, 0), (0, LANES - N_EXPERTS)))
    wr = jnp.concatenate(_split_bf16(wr), axis=-1)
    wg, wu = w_exp_gate[0].astype(BF16), w_exp_up[0].astype(BF16)
    wd = w_exp_down[0].astype(BF16)
    wsgu = jnp.concatenate([w_sh_gate[0], w_sh_up[0]], axis=-1).astype(BF16)
    wsd = w_sh_down[0].astype(BF16)
    row = lambda v: v.reshape(1, -1)

    n_rows = -(-(b + 1) // 8) * 8
    cc = jnp.zeros((n_rows, d), F32).at[:b].set(c).at[b].set(c_ctx)
    mods = _mods(cc, w_mod[0], row(b_mod[0])).reshape(n_rows, N_MOD, d)

    c64, s64 = _rope_tables(s)
    pad = lambda t: np.pad(t, ((0, 0), (0, LANES - D_ROPE)))
    ones_c, zeros_c = np.ones((lc, D_ROPE), np.float32), np.zeros((lc, D_ROPE), np.float32)
    tail = (wqt, wk, wvt, row(q_a_norm[0]), row(kv_a_norm[0]))
    common = (row(attn_norm[0]), win, wcv) + tail
    ctx_only = (row(attn_norm[0]), win, wcv[:, :LANES]) + tail
    kc, vct = _inproj(ctx, mods, lambda bi: b, *ctx_only, pad(ones_c), pad(zeros_c), ones_c.T, zeros_c.T,
                      latent=False, tm=min(TM_IN, lc))
    qt, kx, vxt, gb, u = _inproj(x, mods, lambda bi: bi, *common, pad(c64), pad(s64), c64.T, s64.T,
                                 latent=True, tm=tm_in)

    o_mla = _attention(qt, kc, vct, kx, vxt)

    h, xn, ri, rw, cnt = _post(o_mla, gb, u, x, mods, conv_w[0], row(o_norm_mla[0]), row(o_norm_conv[0]),
                               w_out[0].astype(BF16), row(ffn_norm[0]), wr,
                               router_bias[0].reshape(-1, 1), tm=tm_post)

    tmx = TM_EXPERT
    n_tiles = pl.cdiv(b * s * TOP_K, tmx) + N_EXPERTS
    counts = cnt[:, 0]
    tiles_per = (counts + tmx - 1) // tmx
    tile_end = jnp.cumsum(tiles_per)
    tile_start = tile_end - tiles_per
    base = (tile_start * tmx).astype(jnp.int32)
    tile_ids = jnp.arange(n_tiles, dtype=jnp.int32)
    te = jnp.minimum(jnp.sum(tile_end[None, :] <= tile_ids[:, None], axis=1), N_EXPERTS - 1)
    own = te[:, None] == jnp.arange(N_EXPERTS)[None, :]
    left = jnp.sum(jnp.where(own, counts[None, :] - (tile_ids[:, None] - tile_start[None, :]) * tmx, 0),
                   axis=1)
    tv = jnp.where(tile_ids < tile_end[-1], jnp.clip(left, 0, tmx), 0).astype(jnp.int32)
    tail = tile_end[-1] + jnp.arange(N_EXPERTS)
    zt = jnp.concatenate([jnp.where(tiles_per > 0, tile_end - 1, -1),
                          jnp.where(tail < n_tiles, tail, -1)]).astype(jnp.int32)

    pos = _positions(base, ri, tm=min(TM_POSITIONS, b * s))
    xs = _dispatch(zt, pos, xn, n_rows=n_tiles * tmx, tm=min(TM_DISPATCH, s), tile=tmx)
    ys = _experts(te.astype(jnp.int32), tv, xs, wg, wu, wd, tm=tmx)
    return _combine(pos, rw, xn, h, mods, wsgu, wsd, row(final_norm), ys, tm=min(TM_COMBINE, s))
```

```python
import functools
import math

import jax
import jax.numpy as jnp
import numpy as np
from jax import lax
from jax.experimental import pallas as pl
from jax.experimental.pallas import tpu as pltpu

F32 = jnp.float32
BF16 = jnp.bfloat16

N_HEADS = 8
D_NOPE = 128
D_ROPE = 64
D_V = 128
D_QK_PAD = 256
Q_RANK = 512
KV_RANK = 256
MLA_W = N_HEADS * D_V
GRID_W = 64
ROPE_BASE = 10000.0
ATTN_SCALE = (D_NOPE + D_ROPE) ** -0.5
LOG2E = math.log2(math.e)
N_EXPERTS = 32
N_GROUPS = 4
GROUP_SIZE = N_EXPERTS // N_GROUPS
TOPK_GROUPS = 2
TOP_K = 4
ROUTED_SCALE = 2.5
N_MOD = 6
EPS = 1e-6

C_Q = 0
C_KV = Q_RANK
C_KR = C_KV + KV_RANK
C_GB = C_KR + 128

LANES = 128
BF16_SUBLANES = 16
VMEM_LIMIT = 56 * 1024 * 1024

TM_IN = 512
ATTN_Q_TILES = 8
TM_POST = 512
TM_POSITIONS = 4096
TM_DISPATCH = 256
TM_EXPERT = 512
TM_COMBINE = 256
TN_MODS = 1024
DMA_UNROLL = 8
COMBINE_SLICES = 8


def _rms(x, g):
    return x * lax.rsqrt(jnp.mean(x * x, axis=-1, keepdims=True) + EPS) * g


def _silu(x):
    return x / (1.0 + jnp.exp(-x))


def _split_bf16(x):
    hi = x.astype(BF16)
    lo = (x - hi.astype(F32)).astype(BF16)
    return hi, lo


def _dot(a, b):
    return jnp.dot(a, b, preferred_element_type=F32)


def _mods_kernel(a_ref, w_ref, b_ref, o_ref):
    a = _silu(a_ref[...])
    ah, al = _split_bf16(a)
    wh, wl = _split_bf16(w_ref[...])
    o_ref[...] = _dot(ah, wh) + _dot(al, wh) + _dot(ah, wl) + b_ref[...]


def _mods(cc, w_mod, b_mod):
    rows, d = cc.shape
    n = w_mod.shape[1]
    return pl.pallas_call(
        _mods_kernel,
        grid=(n // TN_MODS,),
        in_specs=[pl.BlockSpec((rows, d), lambda j: (0, 0)),
                  pl.BlockSpec((d, TN_MODS), lambda j: (0, j)),
                  pl.BlockSpec((1, TN_MODS), lambda j: (0, j))],
        out_specs=pl.BlockSpec((rows, TN_MODS), lambda j: (0, j)),
        out_shape=jax.ShapeDtypeStruct((rows, n), F32),
        compiler_params=pltpu.CompilerParams(dimension_semantics=("arbitrary",),
                                             vmem_limit_bytes=VMEM_LIMIT),
        name="mods",
    )(cc, w_mod, b_mod)


def _inproj_kernel(x_ref, mods_ref, g_ref, win_ref, wcv_ref, wqt_ref, wk_ref, wvt_ref, qg_ref, kvg_ref,
                   ca_ref, sa_ref, ct_ref, st_ref, *out_refs, latent):
    x = x_ref[0]
    shift = mods_ref[0, 0:1, :]
    scale = mods_ref[0, 1:2, :]
    nx = (_rms(x, g_ref[...]) * (1.0 + scale) + shift).astype(BF16)

    if latent:
        qt_ref, k_ref, vt_ref, gb_ref, u_ref = out_refs
        q_a = _dot(nx, win_ref[:, C_Q:C_KV])
        qnt = _rms(q_a, qg_ref[...]).T.astype(BF16)
        qt = _dot(wqt_ref[...], qnt) * (ATTN_SCALE * LOG2E)
        ct = ct_ref[...]
        st = st_ref[...]
        for h in range(N_HEADS):
            r0 = h * D_QK_PAD
            r1 = r0 + D_NOPE
            r2 = r1 + D_ROPE
            qt_ref[0, 0, r0:r1, :] = qt[r0:r1].astype(BF16)
            qt_ref[0, 0, r1:r2, :] = (qt[r1:r2] * ct + qt[r2:r0 + D_QK_PAD] * st).astype(BF16)
            qt_ref[0, 0, r2:r0 + D_QK_PAD, :] = jnp.zeros((D_ROPE, qt.shape[1]), BF16)
        gb_ref[0] = _dot(nx, wcv_ref[:, :MLA_W]).astype(BF16)
        g_c = _dot(nx, wcv_ref[:, MLA_W:2 * MLA_W])
        hh = _dot(nx, wcv_ref[:, 2 * MLA_W:])
        u_ref[0] = (g_c * hh).astype(BF16)
    else:
        k_ref, vt_ref = out_refs

    kv_a = _dot(nx, win_ref[:, C_KV:C_KR])
    kvn = _rms(kv_a, kvg_ref[...])
    kn = _dot(kvn.astype(BF16), wk_ref[...])
    vt_ref[0, 0] = _dot(wvt_ref[...], kvn.T.astype(BF16)).astype(BF16)
    blk = _dot(nx, win_ref[:, C_KR:C_GB])
    kr = (blk * ca_ref[...] + pltpu.roll(blk, D_ROPE, axis=1) * sa_ref[...]).astype(BF16)
    for h in range(N_HEADS):
        c0 = h * D_QK_PAD
        k_ref[0, :, c0:c0 + D_NOPE] = kn[:, h * D_NOPE:(h + 1) * D_NOPE].astype(BF16)
        k_ref[0, :, c0 + D_NOPE:c0 + D_QK_PAD] = kr


def _inproj(x, mods, mod_row, g, win, wcv, wqt, wk, wvt, qg, kvg, ca, sa, ct, st, *, latent, tm):
    b, s, d = x.shape
    nt = s // tm
    const = lambda a: pl.BlockSpec(a.shape, lambda bi, i: (0,) * a.ndim, pipeline_mode=pl.Buffered(1))
    tok = lambda w: pl.BlockSpec((1, tm, w), lambda bi, i: (bi, i, 0))
    tok_t = lambda w: pl.BlockSpec((1, 1, w, tm), lambda bi, i: (bi, i, 0, 0))
    in_specs = [tok(d),
                pl.BlockSpec((1, N_MOD, d), lambda bi, i: (mod_row(bi), 0, 0)),
                const(g), const(win), const(wcv), const(wqt), const(wk), const(wvt), const(qg), const(kvg),
                pl.BlockSpec((tm, LANES), lambda bi, i: (i, 0)),
                pl.BlockSpec((tm, LANES), lambda bi, i: (i, 0)),
                pl.BlockSpec((D_ROPE, tm), lambda bi, i: (0, i)),
                pl.BlockSpec((D_ROPE, tm), lambda bi, i: (0, i))]
    kq = N_HEADS * D_QK_PAD
    row_major = lambda w: (tok(w), jax.ShapeDtypeStruct((b, s, w), BF16))
    col_major = lambda w: (tok_t(w), jax.ShapeDtypeStruct((b, nt, w, tm), BF16))
    outs = [row_major(kq), col_major(MLA_W)]
    if latent:
        outs = [col_major(kq)] + outs + [row_major(MLA_W), row_major(MLA_W)]
    return pl.pallas_call(
        functools.partial(_inproj_kernel, latent=latent),
        grid=(b, nt),
        in_specs=in_specs,
        out_specs=[o[0] for o in outs],
        out_shape=[o[1] for o in outs],
        compiler_params=pltpu.CompilerParams(dimension_semantics=("parallel", "arbitrary"),
                                             vmem_limit_bytes=VMEM_LIMIT),
        name="inproj_latent" if latent else "inproj_ctx",
    )(x, mods, g, win, wcv, wqt, wk, wvt, qg, kvg, ca, sa, ct, st)


def _attn_kernel(qt_ref, kc_ref, vct_ref, kx_ref, vxt_ref, o_ref, s_ref):
    qt = jnp.concatenate([qt_ref[0, j] for j in range(qt_ref.shape[1])], axis=1)
    tq = qt.shape[1]
    nb = vxt_ref.shape[1]
    tk = kx_ref.shape[1] // nb
    lc = kc_ref.shape[1]

    def scores(k):
        s = _dot(k, qt)
        return s, jnp.max(s, axis=0, keepdims=True)

    def keys(j):
        return kx_ref[0, pl.ds(pl.multiple_of(j * tk, tk), tk), :]

    def absorb(s, mx, vt, carry):
        m, l, acc = carry
        m_new = jnp.maximum(m, mx)
        alpha = jnp.exp2(m - m_new)
        p = jnp.exp2(s - m_new)
        l = alpha * l + jnp.sum(p, axis=0, keepdims=True)
        acc = alpha * acc + _dot(vt, p.astype(BF16))
        return m_new, l, acc

    def pair(cur_rows, cur_vt, j, carry, mx):
        s_n, mx_n = scores(keys(j))
        s_ref[1] = s_n
        carry = absorb(s_ref[0, :cur_rows, :], mx, cur_vt, carry)
        s_a, mx_a = scores(keys(j + 1))
        s_ref[0] = s_a
        carry = absorb(s_ref[1], mx_n, vxt_ref[0, j], carry)
        return carry, mx_a

    init = (jnp.full((1, tq), -jnp.inf, F32), jnp.zeros((1, tq), F32), jnp.zeros((D_V, tq), F32))
    s, mx = scores(kc_ref[0])
    s_ref[0, :lc, :] = s
    state = pair(lc, vct_ref[0, 0], 0, init, mx)

    def body(i, state):
        return pair(tk, vxt_ref[0, 2 * i - 1], 2 * i, *state)

    carry, mx = lax.fori_loop(1, nb // 2, body, state)
    _, l, acc = absorb(s_ref[0], mx, vxt_ref[0, nb - 1], carry)
    o_ref[0] = (acc / l).T.astype(o_ref.dtype)


def _attention(qt, kc, vct, kx, vxt):
    b, nt, _, tk = qt.shape
    s = kx.shape[1]
    lc = kc.shape[1]
    nq = min(ATTN_Q_TILES, nt)
    tq = nq * tk
    assert nt % 2 == 0 and nt % nq == 0 and lc <= tk, \
        "latent chunks are absorbed in pairs after one context chunk"
    return pl.pallas_call(
        _attn_kernel,
        grid=(b, N_HEADS, nt // nq),
        in_specs=[pl.BlockSpec((1, nq, D_QK_PAD, tk), lambda bi, h, i: (bi, i, h, 0)),
                  pl.BlockSpec((1, lc, D_QK_PAD), lambda bi, h, i: (bi, 0, h)),
                  pl.BlockSpec((1, 1, D_V, lc), lambda bi, h, i: (bi, 0, h, 0)),
                  pl.BlockSpec((1, s, D_QK_PAD), lambda bi, h, i: (bi, 0, h)),
                  pl.BlockSpec((1, nt, D_V, tk), lambda bi, h, i: (bi, 0, h, 0))],
        out_specs=pl.BlockSpec((1, tq, D_V), lambda bi, h, i: (bi, i, h)),
        out_shape=jax.ShapeDtypeStruct((b, s, MLA_W), BF16),
        scratch_shapes=[pltpu.VMEM((2, tk, tq), F32)],
        compiler_params=pltpu.CompilerParams(
            dimension_semantics=("parallel", "parallel", "arbitrary"), vmem_limit_bytes=VMEM_LIMIT),
        name="attn",
    )(qt, kc, vct, kx, vxt)


def _route(logits_t, bias):
    e, tm = logits_t.shape
    neg = -jnp.inf
    s = 1.0 / (1.0 + jnp.exp(-logits_t))
    sel = s + bias
    io_g = lax.broadcasted_iota(jnp.int32, (GROUP_SIZE, tm), 0)
    gs = []
    for g in range(N_GROUPS):
        blk = sel[g * GROUP_SIZE:(g + 1) * GROUP_SIZE]
        m1 = jnp.max(blk, axis=0, keepdims=True)
        i1 = jnp.min(jnp.where(blk == m1, io_g, GROUP_SIZE), axis=0, keepdims=True)
        m2 = jnp.max(jnp.where(io_g == i1, neg, blk), axis=0, keepdims=True)
        gs.append(m1 + m2)
    masks = []
    for g in range(N_GROUPS):
        rank = jnp.zeros((1, tm), jnp.int32)
        for o in range(N_GROUPS):
            if o == g:
                continue
            ahead = (gs[o] > gs[g]) | (gs[o] == gs[g]) if o < g else (gs[o] > gs[g])
            rank = rank + ahead.astype(jnp.int32)
        masks.append(jnp.broadcast_to(rank < TOPK_GROUPS, (GROUP_SIZE, tm)))
    emask = jnp.concatenate(masks, axis=0)
    cur = jnp.where(emask, sel, neg)
    io_e = lax.broadcasted_iota(jnp.int32, (e, tm), 0)
    chosen = jnp.zeros((e, tm), jnp.bool_)
    slots = []
    for _ in range(TOP_K):
        m = jnp.max(cur, axis=0, keepdims=True)
        i = jnp.min(jnp.where(cur == m, io_e, e), axis=0, keepdims=True)
        hit = io_e == i
        chosen = chosen | hit
        cur = jnp.where(hit, neg, cur)
        slots.append((hit, i))
    wsum = jnp.sum(jnp.where(chosen, s, 0.0), axis=0, keepdims=True)
    return chosen, jnp.where(chosen, s / wsum * ROUTED_SCALE, 0.0), slots


def _to_token_tiles(x):
    return x.reshape(x.shape[0], x.shape[1] // LANES, LANES)


def _from_token_tiles(x):
    return x.reshape(x.shape[0], x.shape[1] * x.shape[2])


def _post_kernel(om_ref, gb_ref, u_ref, up_ref, un_ref, x_ref, mods_ref, cw_ref, g1_ref, g2_ref,
                 wout_ref, gf_ref, wr_ref, rb_ref, tri_ref,
                 h_ref, xn_ref, ri_ref, rw_ref, cnt_ref, run_ref):
    first = (pl.program_id(0) == 0) & (pl.program_id(1) == 0)

    @pl.when(first)
    def _():
        run_ref[...] = jnp.zeros_like(run_ref)

    i = pl.program_id(1)
    nt = pl.num_programs(1)
    tm = u_ref.shape[1]
    u = u_ref[0].astype(F32)
    prev = jnp.where(i > 0, up_ref[0].astype(F32)[BF16_SUBLANES - 1:BF16_SUBLANES], 0.0)
    nxt = jnp.where(i < nt - 1, un_ref[0].astype(F32)[0:1], 0.0)
    row = lax.broadcasted_iota(jnp.int32, u.shape, 0)
    u_m1 = jnp.where(row == 0, prev, pltpu.roll(u, 1, axis=0))
    u_p1 = jnp.where(row == tm - 1, nxt, pltpu.roll(u, tm - 1, axis=0))
    cw = cw_ref[...]
    y = gb_ref[0].astype(F32) * (cw[0:1] * u_m1 + cw[1:2] * u + cw[2:3] * u_p1)
    o = jnp.concatenate([_rms(om_ref[0].astype(F32), g1_ref[...]).astype(BF16),
                         _rms(y, g2_ref[...]).astype(BF16)], axis=-1)
    ga = mods_ref[0, 2:3, :]
    sf = mods_ref[0, 3:4, :]
    scf = mods_ref[0, 4:5, :]
    h = x_ref[0] + ga * _dot(o, wout_ref[...])
    h_ref[0] = h
    xn = _rms(h, gf_ref[...]) * (1.0 + scf) + sf
    xh, xl = _split_bf16(xn)
    xn_ref[...] = _to_token_tiles(xh)
    hh_hl = _dot(xh, wr_ref[...])
    logits = hh_hl[:, :LANES] + hh_hl[:, LANES:] + _dot(xl, wr_ref[:, :LANES])
    chosen, comb_t, slots = _route(logits.T[:N_EXPERTS], rb_ref[...])
    sel = jnp.where(chosen, 1.0, 0.0)
    before = _dot(sel.astype(BF16), tri_ref[...]) + run_ref[...]
    run_ref[...] += jnp.sum(sel, axis=1, keepdims=True)
    cnt_ref[...] = jnp.broadcast_to(run_ref[...], cnt_ref.shape).astype(jnp.int32)
    pick = lambda hit, v: jnp.sum(jnp.where(hit, v, 0.0), axis=0, keepdims=True)
    ri_ref[...] = jnp.concatenate(
        [idx for _, idx in slots] + [pick(hit, before).astype(jnp.int32) for hit, _ in slots], axis=0)
    rw_ref[...] = jnp.concatenate(
        [pick(hit, comb_t) for hit, _ in slots] + [jnp.zeros((LANES - TOP_K, tm), F32)], axis=0).T


def _post(o_mla, gb, u, x, mods, conv_w, g1, g2, wout, gf, wr, rb, *, tm):
    b, s, d = x.shape
    nt = s // tm
    hb = tm // BF16_SUBLANES
    last = s // BF16_SUBLANES - 1
    tri = jnp.asarray(np.triu(np.ones((tm, tm), np.float32), k=1), BF16)
    const = lambda a: pl.BlockSpec(a.shape, lambda bi, i: (0,) * a.ndim, pipeline_mode=pl.Buffered(1))
    tok = lambda w: pl.BlockSpec((1, tm, w), lambda bi, i: (bi, i, 0))
    flat = lambda w: pl.BlockSpec((tm, w), lambda bi, i: (bi * nt + i, 0))
    in_specs = [tok(MLA_W), tok(MLA_W), tok(MLA_W),
                pl.BlockSpec((1, BF16_SUBLANES, MLA_W), lambda bi, i: (bi, jnp.maximum(i * hb - 1, 0), 0)),
                pl.BlockSpec((1, BF16_SUBLANES, MLA_W), lambda bi, i: (bi, jnp.minimum((i + 1) * hb, last), 0)),
                tok(d),
                pl.BlockSpec((1, N_MOD, d), lambda bi, i: (bi, 0, 0)),
                const(conv_w), const(g1), const(g2), const(wout), const(gf), const(wr),
                const(rb), const(tri)]
    t = b * s
    return pl.pallas_call(
        _post_kernel,
        grid=(b, nt),
        in_specs=in_specs,
        out_specs=[tok(d), pl.BlockSpec((tm, d // LANES, LANES), lambda bi, i: (bi * nt + i, 0, 0)),
                   pl.BlockSpec((2 * TOP_K, tm), lambda bi, i: (0, bi * nt + i)),
                   flat(LANES),
                   pl.BlockSpec((N_EXPERTS, LANES), lambda bi, i: (0, 0))],
        out_shape=[jax.ShapeDtypeStruct((b, s, d), F32),
                   jax.ShapeDtypeStruct((t, d // LANES, LANES), BF16),
                   jax.ShapeDtypeStruct((2 * TOP_K, t), jnp.int32),
                   jax.ShapeDtypeStruct((t, LANES), F32),
                   jax.ShapeDtypeStruct((N_EXPERTS, LANES), jnp.int32)],
        scratch_shapes=[pltpu.VMEM((N_EXPERTS, 1), F32)],
        compiler_params=pltpu.CompilerParams(dimension_semantics=("arbitrary", "arbitrary"),
                                             vmem_limit_bytes=VMEM_LIMIT),
        name="post",
    )(o_mla, gb, u, u, u, x, mods, conv_w, g1, g2, wout, gf, wr, rb, tri)


def _positions_kernel(base_ref, ri_ref, pos_ref):
    e = ri_ref[0:TOP_K, :]
    pos = ri_ref[TOP_K:2 * TOP_K, :]
    for j in range(N_EXPERTS):
        pos = pos + jnp.where(e == j, base_ref[j], 0)
    pos_ref[...] = pos


def _positions(base, ri, *, tm):
    t = ri.shape[1]
    return pl.pallas_call(
        _positions_kernel,
        grid_spec=pltpu.PrefetchScalarGridSpec(
            num_scalar_prefetch=1,
            grid=(t // tm,),
            in_specs=[pl.BlockSpec((2 * TOP_K, tm), lambda i, base: (0, i))],
            out_specs=pl.BlockSpec((TOP_K, tm), lambda i, base: (0, i))),
        out_shape=jax.ShapeDtypeStruct((TOP_K, t), jnp.int32),
        compiler_params=pltpu.CompilerParams(dimension_semantics=("arbitrary",)),
        name="positions",
    )(base, ri)


def _dispatch_kernel(zt_ref, pos_ref, xn_ref, xs_ref, zero_ref, sem):
    tm = xn_ref.shape[0]
    tz = zero_ref.shape[0]

    @pl.when(pl.program_id(0) == 0)
    def _():
        zero_ref[...] = jnp.zeros_like(zero_ref)
        n_fill = zt_ref.shape[0]
        fill = lambda j: pltpu.make_async_copy(
            zero_ref, xs_ref.at[pl.ds(jnp.maximum(zt_ref[j], 0) * tz, tz)], sem)
        for j in range(n_fill):
            pl.when(zt_ref[j] >= 0)(fill(j).start)
        for j in range(n_fill):
            pl.when(zt_ref[j] >= 0)(fill(j).wait)

    for t in range(tm):
        for k in range(TOP_K):
            pltpu.make_async_copy(xn_ref.at[pl.ds(t, 1)], xs_ref.at[pl.ds(pos_ref[k, t], 1)],
                                  sem).start(priority=k % 2)
    for _ in range(TOP_K):
        pltpu.make_async_copy(xn_ref, xs_ref.at[pl.ds(0, tm)], sem).wait()


def _dispatch(zt, pos, xn, *, n_rows, tm, tile):
    t, *row = xn.shape
    return pl.pallas_call(
        _dispatch_kernel,
        grid_spec=pltpu.PrefetchScalarGridSpec(
            num_scalar_prefetch=1,
            grid=(t // tm,),
            in_specs=[pl.BlockSpec((TOP_K, tm), lambda i, zt: (0, i), memory_space=pltpu.SMEM),
                      pl.BlockSpec((tm, *row), lambda i, zt: (i, 0, 0))],
            out_specs=pl.BlockSpec(memory_space=pl.ANY),
            scratch_shapes=[pltpu.VMEM((tile, *row), xn.dtype), pltpu.SemaphoreType.DMA(())]),
        out_shape=jax.ShapeDtypeStruct((n_rows, *row), xn.dtype),
        compiler_params=pltpu.CompilerParams(dimension_semantics=("arbitrary",),
                                             vmem_limit_bytes=VMEM_LIMIT),
        name="dispatch",
    )(zt, pos, xn)


def _expert_kernel(te_ref, tv_ref, xs_ref, wg_ref, wu_ref, wd_ref, ys_ref):
    nv = tv_ref[pl.program_id(0)]

    @pl.when(nv > 0)
    def _():
        x = _from_token_tiles(xs_ref[...])
        rows = lax.broadcasted_iota(jnp.int32, x.shape, 0)
        x = jnp.where(rows < nv, x, jnp.zeros_like(x))
        a = (_silu(_dot(x, wg_ref[0])) * _dot(x, wu_ref[0])).astype(BF16)
        ys_ref[...] = _to_token_tiles(_dot(a, wd_ref[0]).astype(BF16))

    @pl.when(nv == 0)
    def _():
        ys_ref[...] = jnp.zeros_like(ys_ref)


def _experts(te, tv, xs, wg, wu, wd, *, tm):
    n_rows, *row = xs.shape
    per_expert = lambda a: pl.BlockSpec((1,) + a.shape[1:], lambda i, te, tv: (te[i], 0, 0))
    rows = pl.BlockSpec((tm, *row), lambda i, te, tv: (i, 0, 0))
    return pl.pallas_call(
        _expert_kernel,
        grid_spec=pltpu.PrefetchScalarGridSpec(
            num_scalar_prefetch=2,
            grid=(n_rows // tm,),
            in_specs=[rows, per_expert(wg), per_expert(wu), per_expert(wd)],
            out_specs=rows),
        out_shape=jax.ShapeDtypeStruct(xs.shape, xs.dtype),
        compiler_params=pltpu.CompilerParams(dimension_semantics=("arbitrary",),
                                             vmem_limit_bytes=VMEM_LIMIT),
        name="experts",
    )(te, tv, xs, wg, wu, wd)


def _combine_kernel(pos_ref, pos_next_ref, rw_ref, xn_ref, h_ref, mods_ref, wsgu_ref, wsd_ref, gfin_ref,
                    ys_ref, o_ref, buf_ref, sems):
    tm = xn_ref.shape[0]
    d = h_ref.shape[2]
    g = pl.program_id(0) * pl.num_programs(1) + pl.program_id(1)
    n = pl.num_programs(0) * pl.num_programs(1)

    def start_rows(p_ref, t, to):
        for k in range(TOP_K):
            pltpu.make_async_copy(ys_ref.at[pl.ds(p_ref[k, t], 1)],
                                  buf_ref.at[to, k, pl.ds(t, 1)], sems.at[to]).start(priority=k % 2)

    def wait_rows(to):
        for k in range(TOP_K):
            pltpu.make_async_copy(ys_ref.at[pl.ds(0, tm)], buf_ref.at[to, k], sems.at[to]).wait()

    @pl.when(g == 0)
    def _():
        def issue(t, carry):
            start_rows(pos_ref, t, 0)
            return carry
        lax.fori_loop(0, tm, issue, 0, unroll=DMA_UNROLL)

    def tile(slot):
        other = 1 - slot
        rw = rw_ref[...]
        xb = _from_token_tiles(xn_ref[...])
        kc, tc = d // COMBINE_SLICES, tm // COMBINE_SLICES
        first_fold = COMBINE_SLICES - TOP_K
        sgu = routed = None
        for j in range(COMBINE_SLICES):
            part = _dot(xb[:, j * kc:(j + 1) * kc], wsgu_ref[j * kc:(j + 1) * kc, :])
            sgu = part if sgu is None else sgu + part
            if j == first_fold:
                wait_rows(slot)
            if j >= first_fold:
                k = j - first_fold
                term = rw[:, k:k + 1] * _from_token_tiles(buf_ref[slot, k]).astype(F32)
                routed = term if routed is None else routed + term
            for t in range(j * tc, (j + 1) * tc):
                start_rows(pos_next_ref, t, other)
        sf = sgu.shape[1] // 2
        y = routed + _dot((_silu(sgu[:, :sf]) * sgu[:, sf:]).astype(BF16), wsd_ref[...])
        gate = mods_ref[0, 5:6, :]
        o_ref[0] = _rms(h_ref[0] + gate * y, gfin_ref[...])

        @pl.when(g == n - 1)
        def _():
            wait_rows(other)

    pl.when(g % 2 == 0)(functools.partial(tile, 0))
    pl.when(g % 2 == 1)(functools.partial(tile, 1))


def _combine(pos, rw, xn, h, mods, wsgu, wsd, gfin, ys, *, tm):
    b, s, d = h.shape
    nt = s // tm
    last = b * nt - 1
    const = lambda a: pl.BlockSpec(a.shape, lambda bi, i: (0,) * a.ndim, pipeline_mode=pl.Buffered(1))
    flat = lambda w: pl.BlockSpec((tm, w), lambda bi, i: (bi * nt + i, 0))
    return pl.pallas_call(
        _combine_kernel,
        grid=(b, nt),
        in_specs=[pl.BlockSpec((TOP_K, tm), lambda bi, i: (0, bi * nt + i), memory_space=pltpu.SMEM),
                  pl.BlockSpec((TOP_K, tm), lambda bi, i: (0, jnp.minimum(bi * nt + i + 1, last)),
                               memory_space=pltpu.SMEM),
                  flat(LANES),
                  pl.BlockSpec((tm,) + xn.shape[1:], lambda bi, i: (bi * nt + i, 0, 0)),
                  pl.BlockSpec((1, tm, d), lambda bi, i: (bi, i, 0)),
                  pl.BlockSpec((1, N_MOD, d), lambda bi, i: (bi, 0, 0)),
                  const(wsgu), const(wsd), const(gfin),
                  pl.BlockSpec(memory_space=pl.ANY)],
        out_specs=pl.BlockSpec((1, tm, d), lambda bi, i: (bi, i, 0)),
        out_shape=jax.ShapeDtypeStruct((b, s, d), F32),
        scratch_shapes=[pltpu.VMEM((2, TOP_K, tm) + ys.shape[1:], ys.dtype),
                        pltpu.SemaphoreType.DMA((2,))],
        compiler_params=pltpu.CompilerParams(dimension_semantics=("arbitrary", "arbitrary"),
                                             vmem_limit_bytes=VMEM_LIMIT),
        name="combine",
    )(pos, pos, rw, xn, h, mods, wsgu, wsd, gfin, ys)


def _rope_tables(s):
    nf = D_ROPE // 4
    pos = np.arange(s)
    inv = ROPE_BASE ** (-np.arange(nf, dtype=np.float64) / nf)
    ar = (pos // GRID_W)[:, None] * inv
    ac = (pos % GRID_W)[:, None] * inv
    c64 = np.concatenate([np.cos(ar), np.cos(ar), np.cos(ac), np.cos(ac)], axis=-1)
    s64 = np.concatenate([-np.sin(ar), np.sin(ar), -np.sin(ac), np.sin(ac)], axis=-1)
    return c64.astype(np.float32), s64.astype(np.float32)


def _swap_pairs(w):
    nf = D_ROPE // 4
    return jnp.concatenate([w[..., nf:2 * nf], w[..., :nf], w[..., 3 * nf:], w[..., 2 * nf:3 * nf]], axis=-1)


def kernel(x, c, ctx, c_ctx, w_mod, b_mod, attn_norm, w_in, q_a_norm, w_q_b, kv_a_norm, w_kv_b, conv_w,
           o_norm_mla, o_norm_conv, w_out, ffn_norm, w_router, router_bias, w_exp_gate, w_exp_up,
           w_exp_down, w_sh_gate, w_sh_up, w_sh_down, final_norm):
    b, s, d = x.shape
    lc = ctx.shape[1]
    assert w_mod.shape[0] == 1, "single trunk layer"
    assert s % GRID_W == 0
    assert lc <= TM_IN, "context keys are processed as a single attention chunk"
    tm_in, tm_post = min(TM_IN, s), min(TM_POST, s)

    wi = w_in[0]
    kr_cols = wi[:, C_KR:C_KR + D_ROPE]
    win = jnp.concatenate([wi[:, :C_KR], kr_cols, _swap_pairs(kr_cols)], axis=-1).astype(BF16)
    wcv = wi[:, C_KR + D_ROPE:].astype(BF16)
    wq3 = w_q_b[0].reshape(Q_RANK, N_HEADS, D_NOPE + D_ROPE)
    wq = jnp.concatenate([wq3, _swap_pairs(wq3[..., D_NOPE:])], axis=-1)
    wqt = wq.reshape(Q_RANK, N_HEADS * D_QK_PAD).T.astype(BF16)
    wkv3 = w_kv_b[0].reshape(KV_RANK, N_HEADS, D_NOPE + D_V)
    wk = wkv3[..., :D_NOPE].reshape(KV_RANK, MLA_W).astype(BF16)
    wvt = wkv3[..., D_NOPE:].reshape(KV_RANK, MLA_W).T.astype(BF16)
    wr = jnp.pad(w_router[0], ((0, 0), (0, LANES - N_EXPERTS)))
    wr = jnp.concatenate(_split_bf16(wr), axis=-1)
    wg, wu = w_exp_gate[0].astype(BF16), w_exp_up[0].astype(BF16)
    wd = w_exp_down[0].astype(BF16)
    wsgu = jnp.concatenate([w_sh_gate[0], w_sh_up[0]], axis=-1).astype(BF16)
    wsd = w_sh_down[0].astype(BF16)
    row = lambda v: v.reshape(1, -1)

    n_rows = -(-(b + 1) // 8) * 8
    cc = jnp.zeros((n_rows, d), F32).at[:b].set(c).at[b].set(c_ctx)
    mods = _mods(cc, w_mod[0], row(b_mod[0])).reshape(n_rows, N_MOD, d)

    c64, s64 = _rope_tables(s)
    pad = lambda t: np.pad(t, ((0, 0), (0, LANES - D_ROPE)))
    ones_c, zeros_c = np.ones((lc, D_ROPE), np.float32), np.zeros((lc, D_ROPE), np.float32)
    tail = (wqt, wk, wvt, row(q_a_norm[0]), row(kv_a_norm[0]))
    common = (row(attn_norm[0]), win, wcv) + tail
    ctx_only = (row(attn_norm[0]), win, wcv[:, :LANES]) + tail
    kc, vct = _inproj(ctx, mods, lambda bi: b, *ctx_only, pad(ones_c), pad(zeros_c), ones_c.T, zeros_c.T,
                      latent=False, tm=min(TM_IN, lc))
    qt, kx, vxt, gb, u = _inproj(x, mods, lambda bi: bi, *common, pad(c64), pad(s64), c64.T, s64.T,
                                 latent=True, tm=tm_in)

    o_mla = _attention(qt, kc, vct, kx, vxt)

    h, xn, ri, rw, cnt = _post(o_mla, gb, u, x, mods, conv_w[0], row(o_norm_mla[0]), row(o_norm_conv[0]),
                               w_out[0].astype(BF16), row(ffn_norm[0]), wr,
                               router_bias[0].reshape(-1, 1), tm=tm_post)

    tmx = TM_EXPERT
    n_tiles = pl.cdiv(b * s * TOP_K, tmx) + N_EXPERTS
    counts = cnt[:, 0]
    tiles_per = (counts + tmx - 1) // tmx
    tile_end = jnp.cumsum(tiles_per)
    tile_start = tile_end - tiles_per
    base = (tile_start * tmx).astype(jnp.int32)
    tile_ids = jnp.arange(n_tiles, dtype=jnp.int32)
    te = jnp.minimum(jnp.sum(tile_end[None, :] <= tile_ids[:, None], axis=1), N_EXPERTS - 1)
    own = te[:, None] == jnp.arange(N_EXPERTS)[None, :]
    left = jnp.sum(jnp.where(own, counts[None, :] - (tile_ids[:, None] - tile_start[None, :]) * tmx, 0),
                   axis=1)
    tv = jnp.where(tile_ids < tile_end[-1], jnp.clip(left, 0, tmx), 0).astype(jnp.int32)
    tail = tile_end[-1] + jnp.arange(N_EXPERTS)
    zt = jnp.concatenate([jnp.where(tiles_per > 0, tile_end - 1, -1),
                          jnp.where(tail < n_tiles, tail, -1)]).astype(jnp.int32)

    pos = _positions(base, ri, tm=min(TM_POSITIONS, b * s))
    xs = _dispatch(zt, pos, xn, n_rows=n_tiles * tmx, tm=min(TM_DISPATCH, s), tile=tmx)
    ys = _experts(te.astype(jnp.int32), tv, xs, wg, wu, wd, tm=tmx)
    return _combine(pos, rw, xn, h, mods, wsgu, wsd, row(final_norm), ys, tm=min(TM_COMBINE, s))
```

```python
import functools
import math

import jax
import jax.numpy as jnp
import numpy as np
from jax import lax
from jax.experimental import pallas as pl
from jax.experimental.pallas import tpu as pltpu

F32 = jnp.float32
BF16 = jnp.bfloat16

N_HEADS = 8
D_NOPE = 128
D_ROPE = 64
D_V = 128
D_QK_PAD = 256
Q_RANK = 512
KV_RANK = 256
MLA_W = N_HEADS * D_V
GRID_W = 64
ROPE_BASE = 10000.0
ATTN_SCALE = (D_NOPE + D_ROPE) ** -0.5
LOG2E = math.log2(math.e)
N_EXPERTS = 32
N_GROUPS = 4
GROUP_SIZE = N_EXPERTS // N_GROUPS
TOPK_GROUPS = 2
TOP_K = 4
ROUTED_SCALE = 2.5
N_MOD = 6
EPS = 1e-6

C_Q = 0
C_KV = Q_RANK
C_KR = C_KV + KV_RANK
C_GB = C_KR + 128

LANES = 128
BF16_SUBLANES = 16
VMEM_LIMIT = 56 * 1024 * 1024

TM_IN = 512
ATTN_Q_TILES = 8
TM_POST = 512
TM_POSITIONS = 4096
TM_DISPATCH = 256
TM_EXPERT = 512
TM_COMBINE = 256
TN_MODS = 1024
DMA_UNROLL = 8
COMBINE_SLICES = 8


def _rms(x, g):
    return x * lax.rsqrt(jnp.mean(x * x, axis=-1, keepdims=True) + EPS) * g


def _silu(x):
    return x / (1.0 + jnp.exp(-x))


def _split_bf16(x):
    hi = x.astype(BF16)
    lo = (x - hi.astype(F32)).astype(BF16)
    return hi, lo


def _dot(a, b):
    return jnp.dot(a, b, preferred_element_type=F32)


def _mods_kernel(a_ref, w_ref, b_ref, o_ref):
    a = _silu(a_ref[...])
    ah, al = _split_bf16(a)
    wh, wl = _split_bf16(w_ref[...])
    o_ref[...] = _dot(ah, wh) + _dot(al, wh) + _dot(ah, wl) + b_ref[...]


def _mods(cc, w_mod, b_mod):
    rows, d = cc.shape
    n = w_mod.shape[1]
    return pl.pallas_call(
        _mods_kernel,
        grid=(n // TN_MODS,),
        in_specs=[pl.BlockSpec((rows, d), lambda j: (0, 0)),
                  pl.BlockSpec((d, TN_MODS), lambda j: (0, j)),
                  pl.BlockSpec((1, TN_MODS), lambda j: (0, j))],
        out_specs=pl.BlockSpec((rows, TN_MODS), lambda j: (0, j)),
        out_shape=jax.ShapeDtypeStruct((rows, n), F32),
        compiler_params=pltpu.CompilerParams(dimension_semantics=("arbitrary",),
                                             vmem_limit_bytes=VMEM_LIMIT),
        name="mods",
    )(cc, w_mod, b_mod)


def _inproj_kernel(x_ref, mods_ref, g_ref, win_ref, wcv_ref, wqt_ref, wk_ref, wvt_ref, qg_ref, kvg_ref,
                   ca_ref, sa_ref, ct_ref, st_ref, *out_refs, latent):
    x = x_ref[0]
    shift = mods_ref[0, 0:1, :]
    scale = mods_ref[0, 1:2, :]
    nx = (_rms(x, g_ref[...]) * (1.0 + scale) + shift).astype(BF16)

    if latent:
        qt_ref, k_ref, vt_ref, gb_ref, u_ref = out_refs
        q_a = _dot(nx, win_ref[:, C_Q:C_KV])
        qnt = _rms(q_a, qg_ref[...]).T.astype(BF16)
        qt = _dot(wqt_ref[...], qnt) * (ATTN_SCALE * LOG2E)
        ct = ct_ref[...]
        st = st_ref[...]
        for h in range(N_HEADS):
            r0 = h * D_QK_PAD
            r1 = r0 + D_NOPE
            r2 = r1 + D_ROPE
            qt_ref[0, 0, r0:r1, :] = qt[r0:r1].astype(BF16)
            qt_ref[0, 0, r1:r2, :] = (qt[r1:r2] * ct + qt[r2:r0 + D_QK_PAD] * st).astype(BF16)
            qt_ref[0, 0, r2:r0 + D_QK_PAD, :] = jnp.zeros((D_ROPE, qt.shape[1]), BF16)
        gb_ref[0] = _dot(nx, wcv_ref[:, :MLA_W]).astype(BF16)
        g_c = _dot(nx, wcv_ref[:, MLA_W:2 * MLA_W])
        hh = _dot(nx, wcv_ref[:, 2 * MLA_W:])
        u_ref[0] = (g_c * hh).astype(BF16)
    else:
        k_ref, vt_ref = out_refs

    kv_a = _dot(nx, win_ref[:, C_KV:C_KR])
    kvn = _rms(kv_a, kvg_ref[...])
    kn = _dot(kvn.astype(BF16), wk_ref[...])
    vt_ref[0, 0] = _dot(wvt_ref[...], kvn.T.astype(BF16)).astype(BF16)
    blk = _dot(nx, win_ref[:, C_KR:C_GB])
    kr = (blk * ca_ref[...] + pltpu.roll(blk, D_ROPE, axis=1) * sa_ref[...]).astype(BF16)
    for h in range(N_HEADS):
        c0 = h * D_QK_PAD
        k_ref[0, :, c0:c0 + D_NOPE] = kn[:, h * D_NOPE:(h + 1) * D_NOPE].astype(BF16)
        k_ref[0, :, c0 + D_NOPE:c0 + D_QK_PAD] = kr


def _inproj(x, mods, mod_row, g, win, wcv, wqt, wk, wvt, qg, kvg, ca, sa, ct, st, *, latent, tm):
    b, s, d = x.shape
    nt = s // tm
    const = lambda a: pl.BlockSpec(a.shape, lambda bi, i: (0,) * a.ndim, pipeline_mode=pl.Buffered(1))
    tok = lambda w: pl.BlockSpec((1, tm, w), lambda bi, i: (bi, i, 0))
    tok_t = lambda w: pl.BlockSpec((1, 1, w, tm), lambda bi, i: (bi, i, 0, 0))
    in_specs = [tok(d),
                pl.BlockSpec((1, N_MOD, d), lambda bi, i: (mod_row(bi), 0, 0)),
                const(g), const(win), const(wcv), const(wqt), const(wk), const(wvt), const(qg), const(kvg),
                pl.BlockSpec((tm, LANES), lambda bi, i: (i, 0)),
                pl.BlockSpec((tm, LANES), lambda bi, i: (i, 0)),
                pl.BlockSpec((D_ROPE, tm), lambda bi, i: (0, i)),
                pl.BlockSpec((D_ROPE, tm), lambda bi, i: (0, i))]
    kq = N_HEADS * D_QK_PAD
    row_major = lambda w: (tok(w), jax.ShapeDtypeStruct((b, s, w), BF16))
    col_major = lambda w: (tok_t(w), jax.ShapeDtypeStruct((b, nt, w, tm), BF16))
    outs = [row_major(kq), col_major(MLA_W)]
    if latent:
        outs = [col_major(kq)] + outs + [row_major(MLA_W), row_major(MLA_W)]
    return pl.pallas_call(
        functools.partial(_inproj_kernel, latent=latent),
        grid=(b, nt),
        in_specs=in_specs,
        out_specs=[o[0] for o in outs],
        out_shape=[o[1] for o in outs],
        compiler_params=pltpu.CompilerParams(dimension_semantics=("parallel", "arbitrary"),
                                             vmem_limit_bytes=VMEM_LIMIT),
        name="inproj_latent" if latent else "inproj_ctx",
    )(x, mods, g, win, wcv, wqt, wk, wvt, qg, kvg, ca, sa, ct, st)


def _attn_kernel(qt_ref, kc_ref, vct_ref, kx_ref, vxt_ref, o_ref, s_ref):
    qt = jnp.concatenate([qt_ref[0, j] for j in range(qt_ref.shape[1])], axis=1)
    tq = qt.shape[1]
    nb = vxt_ref.shape[1]
    tk = kx_ref.shape[1] // nb
    lc = kc_ref.shape[1]

    def scores(k):
        s = _dot(k, qt)
        return s, jnp.max(s, axis=0, keepdims=True)

    def keys(j):
        return kx_ref[0, pl.ds(pl.multiple_of(j * tk, tk), tk), :]

    def absorb(s, mx, vt, carry):
        m, l, acc = carry
        m_new = jnp.maximum(m, mx)
        alpha = jnp.exp2(m - m_new)
        p = jnp.exp2(s - m_new)
        l = alpha * l + jnp.sum(p, axis=0, keepdims=True)
        acc = alpha * acc + _dot(vt, p.astype(BF16))
        return m_new, l, acc

    def pair(cur_rows, cur_vt, j, carry, mx):
        s_n, mx_n = scores(keys(j))
        s_ref[1] = s_n
        carry = absorb(s_ref[0, :cur_rows, :], mx, cur_vt, carry)
        s_a, mx_a = scores(keys(j + 1))
        s_ref[0] = s_a
        carry = absorb(s_ref[1], mx_n, vxt_ref[0, j], carry)
        return carry, mx_a

    init = (jnp.full((1, tq), -jnp.inf, F32), jnp.zeros((1, tq), F32), jnp.zeros((D_V, tq), F32))
    s, mx = scores(kc_ref[0])
    s_ref[0, :lc, :] = s
    state = pair(lc, vct_ref[0, 0], 0, init, mx)

    def body(i, state):
        return pair(tk, vxt_ref[0, 2 * i - 1], 2 * i, *state)

    carry, mx = lax.fori_loop(1, nb // 2, body, state)
    _, l, acc = absorb(s_ref[0], mx, vxt_ref[0, nb - 1], carry)
    o_ref[0] = (acc / l).T.astype(o_ref.dtype)


def _attention(qt, kc, vct, kx, vxt):
    b, nt, _, tk = qt.shape
    s = kx.shape[1]
    lc = kc.shape[1]
    nq = min(ATTN_Q_TILES, nt)
    tq = nq * tk
    assert nt % 2 == 0 and nt % nq == 0 and lc <= tk, \
        "latent chunks are absorbed in pairs after one context chunk"
    return pl.pallas_call(
        _attn_kernel,
        grid=(b, N_HEADS, nt // nq),
        in_specs=[pl.BlockSpec((1, nq, D_QK_PAD, tk), lambda bi, h, i: (bi, i, h, 0)),
                  pl.BlockSpec((1, lc, D_QK_PAD), lambda bi, h, i: (bi, 0, h)),
                  pl.BlockSpec((1, 1, D_V, lc), lambda bi, h, i: (bi, 0, h, 0)),
                  pl.BlockSpec((1, s, D_QK_PAD), lambda bi, h, i: (bi, 0, h)),
                  pl.BlockSpec((1, nt, D_V, tk), lambda bi, h, i: (bi, 0, h, 0))],
        out_specs=pl.BlockSpec((1, tq, D_V), lambda bi, h, i: (bi, i, h)),
        out_shape=jax.ShapeDtypeStruct((b, s, MLA_W), BF16),
        scratch_shapes=[pltpu.VMEM((2, tk, tq), F32)],
        compiler_params=pltpu.CompilerParams(
            dimension_semantics=("parallel", "parallel", "arbitrary"), vmem_limit_bytes=VMEM_LIMIT),
        name="attn",
    )(qt, kc, vct, kx, vxt)


def _route(logits_t, bias):
    e, tm = logits_t.shape
    neg = -jnp.inf
    s = 1.0 / (1.0 + jnp.exp(-logits_t))
    sel = s + bias
    io_g = lax.broadcasted_iota(jnp.int32, (GROUP_SIZE, tm), 0)
    gs = []
    for g in range(N_GROUPS):
        blk = sel[g * GROUP_SIZE:(g + 1) * GROUP_SIZE]
        m1 = jnp.max(blk, axis=0, keepdims=True)
        i1 = jnp.min(jnp.where(blk == m1, io_g, GROUP_SIZE), axis=0, keepdims=True)
        m2 = jnp.max(jnp.where(io_g == i1, neg, blk), axis=0, keepdims=True)
        gs.append(m1 + m2)
    masks = []
    for g in range(N_GROUPS):
        rank = jnp.zeros((1, tm), jnp.int32)
        for o in range(N_GROUPS):
            if o == g:
                continue
            ahead = (gs[o] > gs[g]) | (gs[o] == gs[g]) if o < g else (gs[o] > gs[g])
            rank = rank + ahead.astype(jnp.int32)
        masks.append(jnp.broadcast_to(rank < TOPK_GROUPS, (GROUP_SIZE, tm)))
    emask = jnp.concatenate(masks, axis=0)
    cur = jnp.where(emask, sel, neg)
    io_e = lax.broadcasted_iota(jnp.int32, (e, tm), 0)
    chosen = jnp.zeros((e, tm), jnp.bool_)
    slots = []
    for _ in range(TOP_K):
        m = jnp.max(cur, axis=0, keepdims=True)
        i = jnp.min(jnp.where(cur == m, io_e, e), axis=0, keepdims=True)
        hit = io_e == i
        chosen = chosen | hit
        cur = jnp.where(hit, neg, cur)
        slots.append((hit, i))
    wsum = jnp.sum(jnp.where(chosen, s, 0.0), axis=0, keepdims=True)
    return chosen, jnp.where(chosen, s / wsum * ROUTED_SCALE, 0.0), slots


def _to_token_tiles(x):
    return x.reshape(x.shape[0], x.shape[1] // LANES, LANES)


def _from_token_tiles(x):
    return x.reshape(x.shape[0], x.shape[1] * x.shape[2])


def _post_kernel(om_ref, gb_ref, u_ref, up_ref, un_ref, x_ref, mods_ref, cw_ref, g1_ref, g2_ref,
                 wout_ref, gf_ref, wr_ref, rb_ref, tri_ref,
                 h_ref, xn_ref, ri_ref, rw_ref, cnt_ref, run_ref):
    first = (pl.program_id(0) == 0) & (pl.program_id(1) == 0)

    @pl.when(first)
    def _():
        run_ref[...] = jnp.zeros_like(run_ref)

    i = pl.program_id(1)
    nt = pl.num_programs(1)
    tm = u_ref.shape[1]
    u = u_ref[0].astype(F32)
    prev = jnp.where(i > 0, up_ref[0].astype(F32)[BF16_SUBLANES - 1:BF16_SUBLANES], 0.0)
    nxt = jnp.where(i < nt - 1, un_ref[0].astype(F32)[0:1], 0.0)
    row = lax.broadcasted_iota(jnp.int32, u.shape, 0)
    u_m1 = jnp.where(row == 0, prev, pltpu.roll(u, 1, axis=0))
    u_p1 = jnp.where(row == tm - 1, nxt, pltpu.roll(u, tm - 1, axis=0))
    cw = cw_ref[...]
    y = gb_ref[0].astype(F32) * (cw[0:1] * u_m1 + cw[1:2] * u + cw[2:3] * u_p1)
    o = jnp.concatenate([_rms(om_ref[0].astype(F32), g1_ref[...]).astype(BF16),
                         _rms(y, g2_ref[...]).astype(BF16)], axis=-1)
    ga = mods_ref[0, 2:3, :]
    sf = mods_ref[0, 3:4, :]
    scf = mods_ref[0, 4:5, :]
    h = x_ref[0] + ga * _dot(o, wout_ref[...])
    h_ref[0] = h
    xn = _rms(h, gf_ref[...]) * (1.0 + scf) + sf
    xh, xl = _split_bf16(xn)
    xn_ref[...] = _to_token_tiles(xh)
    hh_hl = _dot(xh, wr_ref[...])
    logits = hh_hl[:, :LANES] + hh_hl[:, LANES:] + _dot(xl, wr_ref[:, :LANES])
    chosen, comb_t, slots = _route(logits.T[:N_EXPERTS], rb_ref[...])
    sel = jnp.where(chosen, 1.0, 0.0)
    before = _dot(sel.astype(BF16), tri_ref[...]) + run_ref[...]
    run_ref[...] += jnp.sum(sel, axis=1, keepdims=True)
    cnt_ref[...] = jnp.broadcast_to(run_ref[...], cnt_ref.shape).astype(jnp.int32)
    pick = lambda hit, v: jnp.sum(jnp.where(hit, v, 0.0), axis=0, keepdims=True)
    ri_ref[...] = jnp.concatenate(
        [idx for _, idx in slots] + [pick(hit, before).astype(jnp.int32) for hit, _ in slots], axis=0)
    rw_ref[...] = jnp.concatenate(
        [jnp.broadcast_to(pick(hit, comb_t), (LANES, tm)).T for hit, _ in slots], axis=1)


def _post(o_mla, gb, u, x, mods, conv_w, g1, g2, wout, gf, wr, rb, *, tm):
    b, s, d = x.shape
    nt = s // tm
    hb = tm // BF16_SUBLANES
    last = s // BF16_SUBLANES - 1
    tri = jnp.asarray(np.triu(np.ones((tm, tm), np.float32), k=1), BF16)
    const = lambda a: pl.BlockSpec(a.shape, lambda bi, i: (0,) * a.ndim, pipeline_mode=pl.Buffered(1))
    tok = lambda w: pl.BlockSpec((1, tm, w), lambda bi, i: (bi, i, 0))
    flat = lambda w: pl.BlockSpec((tm, w), lambda bi, i: (bi * nt + i, 0))
    in_specs = [tok(MLA_W), tok(MLA_W), tok(MLA_W),
                pl.BlockSpec((1, BF16_SUBLANES, MLA_W), lambda bi, i: (bi, jnp.maximum(i * hb - 1, 0), 0)),
                pl.BlockSpec((1, BF16_SUBLANES, MLA_W), lambda bi, i: (bi, jnp.minimum((i + 1) * hb, last), 0)),
                tok(d),
                pl.BlockSpec((1, N_MOD, d), lambda bi, i: (bi, 0, 0)),
                const(conv_w), const(g1), const(g2), const(wout), const(gf), const(wr),
                const(rb), const(tri)]
    t = b * s
    return pl.pallas_call(
        _post_kernel,
        grid=(b, nt),
        in_specs=in_specs,
        out_specs=[tok(d), pl.BlockSpec((tm, d // LANES, LANES), lambda bi, i: (bi * nt + i, 0, 0)),
                   pl.BlockSpec((2 * TOP_K, tm), lambda bi, i: (0, bi * nt + i)),
                   flat(TOP_K * LANES),
                   pl.BlockSpec((N_EXPERTS, LANES), lambda bi, i: (0, 0))],
        out_shape=[jax.ShapeDtypeStruct((b, s, d), F32),
                   jax.ShapeDtypeStruct((t, d // LANES, LANES), BF16),
                   jax.ShapeDtypeStruct((2 * TOP_K, t), jnp.int32),
                   jax.ShapeDtypeStruct((t, TOP_K * LANES), F32),
                   jax.ShapeDtypeStruct((N_EXPERTS, LANES), jnp.int32)],
        scratch_shapes=[pltpu.VMEM((N_EXPERTS, 1), F32)],
        compiler_params=pltpu.CompilerParams(dimension_semantics=("arbitrary", "arbitrary"),
                                             vmem_limit_bytes=VMEM_LIMIT),
        name="post",
    )(o_mla, gb, u, u, u, x, mods, conv_w, g1, g2, wout, gf, wr, rb, tri)


def _positions_kernel(base_ref, ri_ref, pos_ref):
    e = ri_ref[0:TOP_K, :]
    pos = ri_ref[TOP_K:2 * TOP_K, :]
    for j in range(N_EXPERTS):
        pos = pos + jnp.where(e == j, base_ref[j], 0)
    pos_ref[...] = pos


def _positions(base, ri, *, tm):
    t = ri.shape[1]
    return pl.pallas_call(
        _positions_kernel,
        grid_spec=pltpu.PrefetchScalarGridSpec(
            num_scalar_prefetch=1,
            grid=(t // tm,),
            in_specs=[pl.BlockSpec((2 * TOP_K, tm), lambda i, base: (0, i))],
            out_specs=pl.BlockSpec((TOP_K, tm), lambda i, base: (0, i))),
        out_shape=jax.ShapeDtypeStruct((TOP_K, t), jnp.int32),
        compiler_params=pltpu.CompilerParams(dimension_semantics=("arbitrary",)),
        name="positions",
    )(base, ri)


def _dispatch_kernel(zt_ref, pos_ref, xn_ref, xs_ref, zero_ref, sem):
    tm = xn_ref.shape[0]
    tz = zero_ref.shape[0]

    @pl.when(pl.program_id(0) == 0)
    def _():
        zero_ref[...] = jnp.zeros_like(zero_ref)
        n_fill = zt_ref.shape[0]
        fill = lambda j: pltpu.make_async_copy(
            zero_ref, xs_ref.at[pl.ds(jnp.maximum(zt_ref[j], 0) * tz, tz)], sem)
        for j in range(n_fill):
            pl.when(zt_ref[j] >= 0)(fill(j).start)
        for j in range(n_fill):
            pl.when(zt_ref[j] >= 0)(fill(j).wait)

    for t in range(tm):
        for k in range(TOP_K):
            pltpu.make_async_copy(xn_ref.at[pl.ds(t, 1)], xs_ref.at[pl.ds(pos_ref[k, t], 1)],
                                  sem).start(priority=k % 2)
    for _ in range(TOP_K):
        pltpu.make_async_copy(xn_ref, xs_ref.at[pl.ds(0, tm)], sem).wait()


def _dispatch(zt, pos, xn, *, n_rows, tm, tile):
    t, *row = xn.shape
    return pl.pallas_call(
        _dispatch_kernel,
        grid_spec=pltpu.PrefetchScalarGridSpec(
            num_scalar_prefetch=1,
            grid=(t // tm,),
            in_specs=[pl.BlockSpec((TOP_K, tm), lambda i, zt: (0, i), memory_space=pltpu.SMEM),
                      pl.BlockSpec((tm, *row), lambda i, zt: (i, 0, 0))],
            out_specs=pl.BlockSpec(memory_space=pl.ANY),
            scratch_shapes=[pltpu.VMEM((tile, *row), xn.dtype), pltpu.SemaphoreType.DMA(())]),
        out_shape=jax.ShapeDtypeStruct((n_rows, *row), xn.dtype),
        compiler_params=pltpu.CompilerParams(dimension_semantics=("arbitrary",),
                                             vmem_limit_bytes=VMEM_LIMIT),
        name="dispatch",
    )(zt, pos, xn)


def _expert_kernel(te_ref, tv_ref, xs_ref, wg_ref, wu_ref, wd_ref, ys_ref):
    nv = tv_ref[pl.program_id(0)]

    @pl.when(nv > 0)
    def _():
        x = _from_token_tiles(xs_ref[...])
        rows = lax.broadcasted_iota(jnp.int32, x.shape, 0)
        x = jnp.where(rows < nv, x, jnp.zeros_like(x))
        a = (_silu(_dot(x, wg_ref[0])) * _dot(x, wu_ref[0])).astype(BF16)
        ys_ref[...] = _to_token_tiles(_dot(a, wd_ref[0]).astype(BF16))

    @pl.when(nv == 0)
    def _():
        ys_ref[...] = jnp.zeros_like(ys_ref)


def _experts(te, tv, xs, wg, wu, wd, *, tm):
    n_rows, *row = xs.shape
    per_expert = lambda a: pl.BlockSpec((1,) + a.shape[1:], lambda i, te, tv: (te[i], 0, 0))
    rows = pl.BlockSpec((tm, *row), lambda i, te, tv: (i, 0, 0))
    return pl.pallas_call(
        _expert_kernel,
        grid_spec=pltpu.PrefetchScalarGridSpec(
            num_scalar_prefetch=2,
            grid=(n_rows // tm,),
            in_specs=[rows, per_expert(wg), per_expert(wu), per_expert(wd)],
            out_specs=rows),
        out_shape=jax.ShapeDtypeStruct(xs.shape, xs.dtype),
        compiler_params=pltpu.CompilerParams(dimension_semantics=("arbitrary",),
                                             vmem_limit_bytes=VMEM_LIMIT),
        name="experts",
    )(te, tv, xs, wg, wu, wd)


def _combine_kernel(pos_ref, pos_next_ref, rw_ref, xn_ref, h_ref, mods_ref, wsgu_ref, wsd_ref, gfin_ref,
                    ys_ref, o_ref, buf_ref, sems):
    tm = xn_ref.shape[0]
    d = h_ref.shape[2]
    g = pl.program_id(0) * pl.num_programs(1) + pl.program_id(1)
    n = pl.num_programs(0) * pl.num_programs(1)

    def start_rows(p_ref, t, to):
        for k in range(TOP_K):
            pltpu.make_async_copy(ys_ref.at[pl.ds(p_ref[k, t], 1)],
                                  buf_ref.at[to, k, pl.ds(t, 1)], sems.at[to]).start(priority=k % 2)

    def wait_rows(to):
        for k in range(TOP_K):
            pltpu.make_async_copy(ys_ref.at[pl.ds(0, tm)], buf_ref.at[to, k], sems.at[to]).wait()

    @pl.when(g == 0)
    def _():
        def issue(t, carry):
            start_rows(pos_ref, t, 0)
            return carry
        lax.fori_loop(0, tm, issue, 0, unroll=DMA_UNROLL)

    def tile(slot):
        other = 1 - slot
        xb = _from_token_tiles(xn_ref[...])
        kc, tc = d // COMBINE_SLICES, tm // COMBINE_SLICES
        sgu = None
        for j in range(COMBINE_SLICES):
            part = _dot(xb[:, j * kc:(j + 1) * kc], wsgu_ref[j * kc:(j + 1) * kc, :])
            sgu = part if sgu is None else sgu + part
            for t in range(j * tc, (j + 1) * tc):
                start_rows(pos_next_ref, t, other)
        sf = sgu.shape[1] // 2
        y = _dot((_silu(sgu[:, :sf]) * sgu[:, sf:]).astype(BF16), wsd_ref[...])

        wait_rows(slot)
        rw = rw_ref[...]
        routed = None
        for k in range(TOP_K):
            w_k = rw[:, k * LANES:(k + 1) * LANES].reshape(tm, 1, LANES)
            term = w_k * buf_ref[slot, k].astype(F32)
            routed = term if routed is None else routed + term
        y = y + _from_token_tiles(routed)
        gate = mods_ref[0, 5:6, :]
        o_ref[0] = _rms(h_ref[0] + gate * y, gfin_ref[...])

        @pl.when(g == n - 1)
        def _():
            wait_rows(other)

    pl.when(g % 2 == 0)(functools.partial(tile, 0))
    pl.when(g % 2 == 1)(functools.partial(tile, 1))


def _combine(pos, rw, xn, h, mods, wsgu, wsd, gfin, ys, *, tm):
    b, s, d = h.shape
    nt = s // tm
    last = b * nt - 1
    const = lambda a: pl.BlockSpec(a.shape, lambda bi, i: (0,) * a.ndim, pipeline_mode=pl.Buffered(1))
    flat = lambda w: pl.BlockSpec((tm, w), lambda bi, i: (bi * nt + i, 0))
    return pl.pallas_call(
        _combine_kernel,
        grid=(b, nt),
        in_specs=[pl.BlockSpec((TOP_K, tm), lambda bi, i: (0, bi * nt + i), memory_space=pltpu.SMEM),
                  pl.BlockSpec((TOP_K, tm), lambda bi, i: (0, jnp.minimum(bi * nt + i + 1, last)),
                               memory_space=pltpu.SMEM),
                  flat(TOP_K * LANES),
                  pl.BlockSpec((tm,) + xn.shape[1:], lambda bi, i: (bi * nt + i, 0, 0)),
                  pl.BlockSpec((1, tm, d), lambda bi, i: (bi, i, 0)),
                  pl.BlockSpec((1, N_MOD, d), lambda bi, i: (bi, 0, 0)),
                  const(wsgu), const(wsd), const(gfin),
                  pl.BlockSpec(memory_space=pl.ANY)],
        out_specs=pl.BlockSpec((1, tm, d), lambda bi, i: (bi, i, 0)),
        out_shape=jax.ShapeDtypeStruct((b, s, d), F32),
        scratch_shapes=[pltpu.VMEM((2, TOP_K, tm) + ys.shape[1:], ys.dtype),
                        pltpu.SemaphoreType.DMA((2,))],
        compiler_params=pltpu.CompilerParams(dimension_semantics=("arbitrary", "arbitrary"),
                                             vmem_limit_bytes=VMEM_LIMIT),
        name="combine",
    )(pos, pos, rw, xn, h, mods, wsgu, wsd, gfin, ys)


def _rope_tables(s):
    nf = D_ROPE // 4
    pos = np.arange(s)
    inv = ROPE_BASE ** (-np.arange(nf, dtype=np.float64) / nf)
    ar = (pos // GRID_W)[:, None] * inv
    ac = (pos % GRID_W)[:, None] * inv
    c64 = np.concatenate([np.cos(ar), np.cos(ar), np.cos(ac), np.cos(ac)], axis=-1)
    s64 = np.concatenate([-np.sin(ar), np.sin(ar), -np.sin(ac), np.sin(ac)], axis=-1)
    return c64.astype(np.float32), s64.astype(np.float32)


def _swap_pairs(w):
    nf = D_ROPE // 4
    return jnp.concatenate([w[..., nf:2 * nf], w[..., :nf], w[..., 3 * nf:], w[..., 2 * nf:3 * nf]], axis=-1)


def kernel(x, c, ctx, c_ctx, w_mod, b_mod, attn_norm, w_in, q_a_norm, w_q_b, kv_a_norm, w_kv_b, conv_w,
           o_norm_mla, o_norm_conv, w_out, ffn_norm, w_router, router_bias, w_exp_gate, w_exp_up,
           w_exp_down, w_sh_gate, w_sh_up, w_sh_down, final_norm):
    b, s, d = x.shape
    lc = ctx.shape[1]
    assert w_mod.shape[0] == 1, "single trunk layer"
    assert s % GRID_W == 0
    assert lc <= TM_IN, "context keys are processed as a single attention chunk"
    tm_in, tm_post = min(TM_IN, s), min(TM_POST, s)

    wi = w_in[0]
    kr_cols = wi[:, C_KR:C_KR + D_ROPE]
    win = jnp.concatenate([wi[:, :C_KR], kr_cols, _swap_pairs(kr_cols)], axis=-1).astype(BF16)
    wcv = wi[:, C_KR + D_ROPE:].astype(BF16)
    wq3 = w_q_b[0].reshape(Q_RANK, N_HEADS, D_NOPE + D_ROPE)
    wq = jnp.concatenate([wq3, _swap_pairs(wq3[..., D_NOPE:])], axis=-1)
    wqt = wq.reshape(Q_RANK, N_HEADS * D_QK_PAD).T.astype(BF16)
    wkv3 = w_kv_b[0].reshape(KV_RANK, N_HEADS, D_NOPE + D_V)
    wk = wkv3[..., :D_NOPE].reshape(KV_RANK, MLA_W).astype(BF16)
    wvt = wkv3[..., D_NOPE:].reshape(KV_RANK, MLA_W).T.astype(BF16)
    wr = jnp.pad(w_router[0], ((0, 0), (0, LANES - N_EXPERTS)))
    wr = jnp.concatenate(_split_bf16(wr), axis=-1)
    wg, wu = w_exp_gate[0].astype(BF16), w_exp_up[0].astype(BF16)
    wd = w_exp_down[0].astype(BF16)
    wsgu = jnp.concatenate([w_sh_gate[0], w_sh_up[0]], axis=-1).astype(BF16)
    wsd = w_sh_down[0].astype(BF16)
    row = lambda v: v.reshape(1, -1)

    n_rows = -(-(b + 1) // 8) * 8
    cc = jnp.zeros((n_rows, d), F32).at[:b].set(c).at[b].set(c_ctx)
    mods = _mods(cc, w_mod[0], row(b_mod[0])).reshape(n_rows, N_MOD, d)

    c64, s64 = _rope_tables(s)
    pad = lambda t: np.pad(t, ((0, 0), (0, LANES - D_ROPE)))
    ones_c, zeros_c = np.ones((lc, D_ROPE), np.float32), np.zeros((lc, D_ROPE), np.float32)
    tail = (wqt, wk, wvt, row(q_a_norm[0]), row(kv_a_norm[0]))
    common = (row(attn_norm[0]), win, wcv) + tail
    ctx_only = (row(attn_norm[0]), win, wcv[:, :LANES]) + tail
    kc, vct = _inproj(ctx, mods, lambda bi: b, *ctx_only, pad(ones_c), pad(zeros_c), ones_c.T, zeros_c.T,
                      latent=False, tm=min(TM_IN, lc))
    qt, kx, vxt, gb, u = _inproj(x, mods, lambda bi: bi, *common, pad(c64), pad(s64), c64.T, s64.T,
                                 latent=True, tm=tm_in)

    o_mla = _attention(qt, kc, vct, kx, vxt)

    h, xn, ri, rw, cnt = _post(o_mla, gb, u, x, mods, conv_w[0], row(o_norm_mla[0]), row(o_norm_conv[0]),
                               w_out[0].astype(BF16), row(ffn_norm[0]), wr,
                               router_bias[0].reshape(-1, 1), tm=tm_post)

    tmx = TM_EXPERT
    n_tiles = pl.cdiv(b * s * TOP_K, tmx) + N_EXPERTS
    counts = cnt[:, 0]
    tiles_per = (counts + tmx - 1) // tmx
    tile_end = jnp.cumsum(tiles_per)
    tile_start = tile_end - tiles_per
    base = (tile_start * tmx).astype(jnp.int32)
    tile_ids = jnp.arange(n_tiles, dtype=jnp.int32)
    te = jnp.minimum(jnp.sum(tile_end[None, :] <= tile_ids[:, None], axis=1), N_EXPERTS - 1)
    own = te[:, None] == jnp.arange(N_EXPERTS)[None, :]
    left = jnp.sum(jnp.where(own, counts[None, :] - (tile_ids[:, None] - tile_start[None, :]) * tmx, 0),
                   axis=1)
    tv = jnp.where(tile_ids < tile_end[-1], jnp.clip(left, 0, tmx), 0).astype(jnp.int32)
    tail = tile_end[-1] + jnp.arange(N_EXPERTS)
    zt = jnp.concatenate([jnp.where(tiles_per > 0, tile_end - 1, -1),
                          jnp.where(tail < n_tiles, tail, -1)]).astype(jnp.int32)

    pos = _positions(base, ri, tm=min(TM_POSITIONS, b * s))
    xs = _dispatch(zt, pos, xn, n_rows=n_tiles * tmx, tm=min(TM_DISPATCH, s), tile=tmx)
    ys = _experts(te.astype(jnp.int32), tv, xs, wg, wu, wd, tm=tmx)
    return _combine(pos, rw, xn, h, mods, wsgu, wsd, row(final_norm), ys, tm=min(TM_COMBINE, s))
```
